```python
import jax, jax.numpy as jnp
from jax import lax
import numpy as np

D_MODEL = 4096
BATCH = 1
SEQ = 16384
DEPTH = 1

PLE_DIM = 256
HG_DK = 128
HG_DV = 128
HG_HEADS = (D_MODEL // 2) // HG_DK
HG_W = HG_HEADS * HG_DK
HG_VW = HG_HEADS * HG_DV
GD_DK = 128
GD_DV = 128
GD_HEADS = (D_MODEL // 2) // GD_DV
GD_KW = GD_HEADS * GD_DK
GD_VW = GD_HEADS * GD_DV
GD_QKV = 2 * GD_KW + GD_VW
GD_CONV = 5
CHUNK = 64
MIX_WIDTH = HG_VW + GD_VW
IN_SPLITS = (HG_W, HG_W, HG_W, HG_VW, HG_VW, GD_QKV, 2 * GD_HEADS, 2 * GD_HEADS, GD_VW)
IN_WIDTH = sum(IN_SPLITS)
PEER_HEADS = 8
PEER_NKEYS = 128
PEER_KEY_DIM = 128
PEER_QUERY_DIM = 2 * PEER_KEY_DIM
PEER_TOPK = 16
PEER_N_EXPERTS = PEER_NKEYS * PEER_NKEYS
PEER_BLOCK = 64
EPS = 1e-6

kernel_name = "hymba_hgrn2_gdn_peer_encoder"


def rmsnorm(x, w):
    xf = x.astype(jnp.float32)
    y = xf * lax.rsqrt(jnp.mean(xf * xf, axis=-1, keepdims=True) + EPS)
    return (y * w.astype(jnp.float32)).astype(x.dtype)


def l2norm(t):
    return t * lax.rsqrt(jnp.sum(t * t, axis=-1, keepdims=True) + EPS)


def heads(t, n):
    b, l, w = t.shape
    return t.reshape(b, l, n, w // n).transpose(0, 2, 1, 3)


def flip_seq(t):
    return jnp.flip(t, axis=2)


def to_chunks(t):
    b, h, l = t.shape[:3]
    t = t.reshape(b, h, l // CHUNK, CHUNK, *t.shape[3:])
    return jnp.moveaxis(t, 2, 0)


def from_chunks(t):
    n, b, h, c = t.shape[:4]
    return jnp.moveaxis(t, 0, 2).reshape(b, h, n * c, *t.shape[4:])


def centred_depthwise_conv(x, w):
    k, c = w.shape
    pad = (k - 1) // 2
    return lax.conv_general_dilated(
        x, w[:, None, :].astype(x.dtype), window_strides=(1,),
        padding=[(pad, k - 1 - pad)], dimension_numbers=('NWC', 'WIO', 'NWC'),
        feature_group_count=c)


def hgrn2_scan(q, k, v, log_f):
    b_, h_, _, dk = q.shape
    dv = v.shape[-1]
    qc, kc, vc = to_chunks(q), to_chunks(k), to_chunks(v)
    bcum = jnp.cumsum(to_chunks(log_f), axis=-2)
    incl = jnp.tril(jnp.ones((CHUNK, CHUNK), bool))

    def step(S, inp):
        qi, ki, vi, bi = inp
        rel = jnp.exp(jnp.where(incl[:, :, None],
                                bi[..., :, None, :] - bi[..., None, :, :], -jnp.inf))
        attn = jnp.sum(qi[..., :, None, :] * ki[..., None, :, :] * rel, axis=-1)
        b_last = bi[..., -1:, :]
        o = (jnp.einsum('bhts,bhsv->bhtv', attn, vi)
             + jnp.einsum('bhtk,bhkv->bhtv', qi * jnp.exp(bi), S))
        S = (S * jnp.swapaxes(jnp.exp(b_last), -1, -2)
             + jnp.einsum('bhsk,bhsv->bhkv', ki * jnp.exp(b_last - bi), vi))
        return S, o

    S0 = jnp.zeros((b_, h_, dk, dv), jnp.float32)
    _, o = lax.scan(step, S0, (qc, kc, vc, bcum))
    return from_chunks(o)


def gdn_scan(q, k, v, g, beta):
    b_, h_, _, dk = q.shape
    dv = v.shape[-1]
    qc, kc, vc = to_chunks(q), to_chunks(k), to_chunks(v)
    gam = jnp.cumsum(to_chunks(g), axis=-1)
    bc = to_chunks(beta)
    incl = jnp.tril(jnp.ones((CHUNK, CHUNK), bool))
    strict = jnp.tril(jnp.ones((CHUNK, CHUNK), bool), -1)
    rel = jnp.exp(jnp.where(incl, gam[..., :, None] - gam[..., None, :], -jnp.inf))
    kk = jnp.einsum('nbhid,nbhjd->nbhij', kc, kc)
    a_mat = jnp.where(strict, bc[..., :, None] * kk * rel, 0.0) + jnp.eye(CHUNK, dtype=jnp.float32)
    rhs = jnp.concatenate([vc * bc[..., None], kc * (bc * jnp.exp(gam))[..., None]], axis=-1)
    sol = lax.linalg.triangular_solve(a_mat, rhs, left_side=True, lower=True, unit_diagonal=True)
    u_c, w_c = sol[..., :dv], sol[..., dv:]
    qk = jnp.einsum('nbhid,nbhjd->nbhij', qc, kc) * rel
    q_dec = qc * jnp.exp(gam)[..., None]
    k_dec = kc * jnp.exp(gam[..., -1:] - gam)[..., None]
    g_last = jnp.exp(gam[..., -1])

    def step(S, inp):
        ui, wi, qki, qdi, kdi, gli = inp
        v_new = ui - jnp.einsum('bhck,bhkv->bhcv', wi, S)
        o = jnp.einsum('bhck,bhkv->bhcv', qdi, S) + jnp.einsum('bhij,bhjv->bhiv', qki, v_new)
        S = S * gli[..., None, None] + jnp.einsum('bhck,bhcv->bhkv', kdi, v_new)
        return S, o

    S0 = jnp.zeros((b_, h_, dk, dv), jnp.float32)
    _, o = lax.scan(step, S0, (u_c, w_c, qk, q_dec, k_dec, g_last))
    return from_chunks(o)


def gated_head_norm(o, z, w):
    b_, h_, l_, dv = o.shape
    y = rmsnorm(jnp.swapaxes(o, 1, 2), w) * jax.nn.silu(z.astype(jnp.float32)).reshape(b_, l_, h_, dv)
    return y.reshape(b_, l_, h_ * dv).astype(z.dtype)


def peer(xn, w_query, sub_keys, expert_u, expert_v):
    b_, l_, d_ = xn.shape
    t_ = b_ * l_
    xt = xn.reshape(t_, d_)
    qry = (xt @ w_query).reshape(t_, PEER_HEADS, 2, PEER_KEY_DIM)
    scores = jnp.einsum('thpd,hpkd->thpk', qry, sub_keys).astype(jnp.float32)
    s_top, i_top = lax.top_k(scores, PEER_TOPK)
    cand = (s_top[..., 0, :, None] + s_top[..., 1, None, :]).reshape(t_, PEER_HEADS, PEER_TOPK * PEER_TOPK)
    c_top, c_idx = lax.top_k(cand, PEER_TOPK)
    i1 = jnp.take_along_axis(i_top[..., 0, :], c_idx // PEER_TOPK, axis=-1)
    i2 = jnp.take_along_axis(i_top[..., 1, :], c_idx % PEER_TOPK, axis=-1)
    idx = (i1 * PEER_NKEYS + i2).reshape(t_, PEER_HEADS * PEER_TOPK)
    gates = jax.nn.softmax(c_top, axis=-1).reshape(t_, PEER_HEADS * PEER_TOPK)
    nb = t_ // PEER_BLOCK

    def block(args):
        xc, ic, gc = args
        u = expert_u[ic]
        hid = jnp.einsum('cd,ced->ce', xc, u).astype(jnp.float32)
        act = (jax.nn.gelu(hid, approximate=False) * gc).astype(xc.dtype)
        return jnp.einsum('ce,ced->cd', act, expert_v[ic])

    out = lax.map(block, (xt.reshape(nb, PEER_BLOCK, d_),
                          idx.reshape(nb, PEER_BLOCK, -1),
                          gates.reshape(nb, PEER_BLOCK, -1)))
    return out.reshape(b_, l_, d_)


def setup_inputs(seed: int = 0) -> dict:
    key = jax.random.key(seed)
    ks = jax.random.split(key, 24)
    f32 = jnp.float32
    nrm = lambda k, s, sc: jax.random.normal(k, s, f32) * sc
    gain = lambda k, s: 1.0 + 0.02 * jax.random.normal(k, s, f32)
    dt = jnp.exp(jax.random.uniform(ks[7], (DEPTH, 2, GD_HEADS), f32, np.log(1e-3), np.log(1e-1)))
    return {
        "x": nrm(ks[0], (BATCH, SEQ, D_MODEL), 1.0),
        "p": nrm(ks[1], (DEPTH, BATCH, SEQ, PLE_DIM), 1.0),
        "attn_norm_w": gain(ks[2], (DEPTH, D_MODEL)),
        "w_in": nrm(ks[3], (DEPTH, D_MODEL, IN_WIDTH), D_MODEL ** -0.5),
        "hg_lower_bound": nrm(ks[4], (DEPTH + 1, 2, HG_W), 0.1),
        "gd_conv_w": nrm(ks[5], (DEPTH, GD_CONV, GD_QKV), GD_CONV ** -0.5),
        "gd_A_log": jnp.log(jax.random.uniform(ks[6], (DEPTH, 2, GD_HEADS), f32, 1.0, 16.0)),
        "gd_dt_bias": dt + jnp.log(-jnp.expm1(-dt)),
        "hg_out_norm_w": gain(ks[8], (DEPTH, HG_DV)),
        "gd_out_norm_w": gain(ks[9], (DEPTH, GD_DV)),
        "w_out": nrm(ks[10], (DEPTH, MIX_WIDTH, D_MODEL), MIX_WIDTH ** -0.5),
        "ffn_norm_w": gain(ks[11], (DEPTH, D_MODEL)),
        "peer_w_query": nrm(ks[12], (DEPTH, D_MODEL, PEER_HEADS * PEER_QUERY_DIM), D_MODEL ** -0.5),
        "peer_sub_keys": nrm(ks[13], (DEPTH, PEER_HEADS, 2, PEER_NKEYS, PEER_KEY_DIM), PEER_KEY_DIM ** -0.5),
        "peer_u": nrm(ks[14], (DEPTH, PEER_N_EXPERTS, D_MODEL), D_MODEL ** -0.5),
        "peer_v": nrm(ks[15], (DEPTH, PEER_N_EXPERTS, D_MODEL), PEER_HEADS ** -0.5),
        "ple_norm_w": gain(ks[16], (DEPTH, D_MODEL)),
        "ple_w_gate": nrm(ks[17], (DEPTH, D_MODEL, D_MODEL), D_MODEL ** -0.5),
        "ple_w_proj": nrm(ks[18], (DEPTH, PLE_DIM, D_MODEL), PLE_DIM ** -0.5),
        "ple_post_norm_w": gain(ks[19], (DEPTH, D_MODEL)),
        "final_norm_w": gain(ks[20], (D_MODEL,)),
    }


def reference(x, p, attn_norm_w, w_in, hg_lower_bound, gd_conv_w, gd_A_log, gd_dt_bias,
              hg_out_norm_w, gd_out_norm_w, w_out, ffn_norm_w, peer_w_query, peer_sub_keys,
              peer_u, peer_v, ple_norm_w, ple_w_gate, ple_w_proj, ple_post_norm_w, final_norm_w):
    f32 = jnp.float32
    split_pts = []
    acc = 0
    for w in IN_SPLITS[:-1]:
        acc += w
        split_pts.append(acc)
    lb_all = jnp.cumsum(jax.nn.softmax(hg_lower_bound.astype(f32), axis=0), axis=0)
    h = x
    for i in range(DEPTH):
        xn = rmsnorm(h, attn_norm_w[i])
        proj = xn @ w_in[i]
        hq, hf_fw, hf_bw, hi, hz, gqkv, ga, gb, gz = jnp.split(proj, split_pts, axis=-1)

        lb = lb_all[i]
        q_a = heads(jax.nn.silu(hq.astype(f32)), HG_HEADS) * (HG_DK ** -0.5)
        v_a = heads(hi.astype(f32), HG_HEADS)
        zf_fw, zf_bw = hf_fw.astype(f32), hf_bw.astype(f32)
        logf_fw = jnp.logaddexp(jnp.log(lb[0]), jnp.log1p(-lb[0]) + jax.nn.log_sigmoid(zf_fw))
        logf_bw = jnp.logaddexp(jnp.log(lb[1]), jnp.log1p(-lb[1]) + jax.nn.log_sigmoid(zf_bw))
        k_fw = (1.0 - lb[0]) * jax.nn.sigmoid(-zf_fw)
        k_bw = (1.0 - lb[1]) * jax.nn.sigmoid(-zf_bw)
        o_a = (hgrn2_scan(q_a, heads(k_fw, HG_HEADS), v_a, heads(logf_fw, HG_HEADS))
               + flip_seq(hgrn2_scan(flip_seq(q_a), flip_seq(heads(k_bw, HG_HEADS)),
                                     flip_seq(v_a), flip_seq(heads(logf_bw, HG_HEADS)))))
        out_a = gated_head_norm(o_a, hz, hg_out_norm_w[i])

        qkv = jax.nn.silu(centred_depthwise_conv(gqkv, gd_conv_w[i]).astype(f32))
        q_b = l2norm(heads(qkv[..., :GD_KW], GD_HEADS)) * (GD_DK ** -0.5)
        k_b = l2norm(heads(qkv[..., GD_KW:2 * GD_KW], GD_HEADS))
        v_b = heads(qkv[..., 2 * GD_KW:], GD_HEADS)
        bl = ga.shape[:2]
        a_in = ga.astype(f32).reshape(*bl, 2, GD_HEADS)
        dec = -jnp.exp(gd_A_log[i].astype(f32)) * jax.nn.softplus(a_in + gd_dt_bias[i].astype(f32))
        dec = jnp.transpose(dec, (2, 0, 3, 1))
        beta = jnp.transpose(jax.nn.sigmoid(gb.astype(f32).reshape(*bl, 2, GD_HEADS)), (2, 0, 3, 1))
        o_b = (gdn_scan(q_b, k_b, v_b, dec[0], beta[0])
               + flip_seq(gdn_scan(flip_seq(q_b), flip_seq(k_b), flip_seq(v_b),
                                   flip_seq(dec[1]), flip_seq(beta[1]))))
        out_b = gated_head_norm(o_b, gz, gd_out_norm_w[i])

        mix = jnp.concatenate([out_a, out_b], axis=-1).astype(h.dtype) @ w_out[i]
        h = h + mix

        h = h + peer(rmsnorm(h, ffn_norm_w[i]), peer_w_query[i], peer_sub_keys[i], peer_u[i], peer_v[i])

        gate = jax.nn.sigmoid((rmsnorm(h, ple_norm_w[i]) @ ple_w_gate[i]).astype(f32))
        emb = rmsnorm(p[i].astype(h.dtype) @ ple_w_proj[i], ple_post_norm_w[i]).astype(f32)
        h = h + (gate * emb).astype(h.dtype)
    return rmsnorm(h, final_norm_w)
```

```python
import functools

import jax
import jax.numpy as jnp
from jax import lax
from jax.experimental import pallas as pl
from jax.experimental.pallas import tpu as pltpu

F32 = jnp.float32
BF16 = jnp.bfloat16
EPS = 1e-6
HEAD_DIM = 128
CHUNK = 64
SUB = 16
PEER_TOPK = 16
PEER_NKEYS = 128
LANES = 128
VMEM_LIMIT_BYTES = 56 * 1024 * 1024
HIGHEST = lax.Precision.HIGHEST
NT_DIMS = (((1,), (1,)), ((), ()))
TN_DIMS = (((0,), (0,)), ((), ()))
INV_SQRT2 = 0.7071067811865476


def _params(n_axes):
    return pltpu.CompilerParams(dimension_semantics=("arbitrary",) * n_axes,
                                vmem_limit_bytes=VMEM_LIMIT_BYTES)


def _sigmoid(x):
    return 1.0 / (1.0 + jnp.exp(-x))


def _pick(n, prefs):
    for t in prefs:
        if n % t == 0:
            return t
    return n


def _rms_body(x_ref, w_ref, o_ref):
    x = x_ref[...]
    ms = jnp.mean(x * x, axis=-1, keepdims=True)
    o_ref[...] = (x * lax.rsqrt(ms + EPS) * w_ref[...]).astype(o_ref.dtype)


def _rmsnorm(x, w):
    t, d = x.shape
    tm = _pick(t, (512, 256, 128, 64, 8))
    return pl.pallas_call(
        _rms_body,
        grid=(t // tm,),
        in_specs=[pl.BlockSpec((tm, d), lambda i: (i, 0)),
                  pl.BlockSpec((1, d), lambda i: (0, 0))],
        out_specs=pl.BlockSpec((tm, d), lambda i: (i, 0)),
        out_shape=jax.ShapeDtypeStruct((t, d), BF16),
        compiler_params=_params(1),
        name="rmsnorm",
    )(x, w.reshape(1, d))


def _mm_body(a_ref, b_ref, o_ref):
    o_ref[...] = jnp.dot(a_ref[...], b_ref[...], preferred_element_type=F32)


def _mm_res_body(a_ref, b_ref, r_ref, o_ref):
    o_ref[...] = r_ref[...] + jnp.dot(a_ref[...], b_ref[...], preferred_element_type=F32)


def _matmul(a, b, residual=None, name="matmul"):
    m, k = a.shape
    n = b.shape[1]
    tm = _pick(m, (1024, 512, 256, 128, 64, 8))
    tn = _pick(n, (512, 256, 128))
    in_specs = [pl.BlockSpec((tm, k), lambda i, j: (i, 0)),
                pl.BlockSpec((k, tn), lambda i, j: (0, j))]
    args = [a, b]
    body = _mm_body
    if residual is not None:
        in_specs.append(pl.BlockSpec((tm, tn), lambda i, j: (i, j)))
        args.append(residual)
        body = _mm_res_body
    return pl.pallas_call(
        body,
        grid=(m // tm, n // tn),
        in_specs=in_specs,
        out_specs=pl.BlockSpec((tm, tn), lambda i, j: (i, j)),
        out_shape=jax.ShapeDtypeStruct((m, n), F32),
        compiler_params=_params(2),
        name=name,
    )(*args)


def _tri(reverse, strict=False):
    r = lax.broadcasted_iota(jnp.int32, (CHUNK, CHUNK), 0)
    c = lax.broadcasted_iota(jnp.int32, (CHUNK, CHUNK), 1)
    if reverse:
        return (c > r) if strict else (c >= r)
    return (c < r) if strict else (c <= r)


def _hgrn_body(hq_ref, hf_ref, hi_ref, lbp_ref, o_ref, st_ref, *, nh, reverse):
    @pl.when(pl.program_id(0) == 0)
    def _():
        st_ref[...] = jnp.zeros_like(st_ref)

    nl = lbp_ref.shape[0]
    mx = lbp_ref[0]
    for l in range(1, nl):
        mx = jnp.maximum(mx, lbp_ref[l])
    den = jnp.exp(lbp_ref[0] - mx)
    e0 = den
    for l in range(1, nl):
        den = den + jnp.exp(lbp_ref[l] - mx)
    lb = e0 / den

    cum = _tri(reverse).astype(F32)
    col = lax.broadcasted_iota(jnp.int32, (SUB, CHUNK), 1)
    row16 = lax.broadcasted_iota(jnp.int32, (SUB, 1), 0)
    nsub = CHUNK // SUB
    last = 0 if reverse else CHUNK - 1

    for h in range(nh):
        sl = slice(h * HEAD_DIM, (h + 1) * HEAD_DIM)
        zq = hq_ref[:, sl]
        zf = hf_ref[:, sl]
        v = hi_ref[:, sl]
        lbh = lb[:, sl]
        q = zq * _sigmoid(zq) * (HEAD_DIM ** -0.5)
        s = _sigmoid(zf)
        logf = jnp.log(lbh + (1.0 - lbh) * s)
        k = (1.0 - lbh) * (1.0 - s)
        b = jnp.dot(cum, logf, precision=HIGHEST, preferred_element_type=F32)
        b_last = b[last:last + 1]
        vb = v.astype(BF16)
        st = st_ref[h]
        o = lax.dot_general((q * jnp.exp(b)).astype(BF16), st.astype(BF16), NT_DIMS,
                            preferred_element_type=F32)

        rows = []
        for i in range(nsub):
            first = (i == nsub - 1) if reverse else (i == 0)
            if first:
                rows.append(jnp.zeros((SUB, CHUNK), F32))
                continue
            rb = SUB * (i + 1) if reverse else SUB * i - 1
            ref = b[rb:rb + 1]
            qt = (q[SUB * i:SUB * (i + 1)] * jnp.exp(b[SUB * i:SUB * (i + 1)] - ref)).astype(BF16)
            kh = (k * jnp.exp(jnp.minimum(ref - b, 0.0))).astype(BF16)
            a = lax.dot_general(qt, kh, NT_DIMS, preferred_element_type=F32)
            keep = (col >= SUB * (i + 1)) if reverse else (col < SUB * i)
            rows.append(jnp.where(keep, a, 0.0))
        a_cross = jnp.concatenate(rows, axis=0)
        o = o + jnp.dot(a_cross.astype(BF16), vb, preferred_element_type=F32)

        diag = []
        for j in range(nsub):
            qj = q[SUB * j:SUB * (j + 1)]
            bj = b[SUB * j:SUB * (j + 1)]
            oj = jnp.zeros((SUB, HEAD_DIM), F32)
            for s_loc in range(SUB):
                r = SUB * j + s_loc
                pr = qj * k[r:r + 1] * jnp.exp(bj - b[r:r + 1])
                a = jnp.sum(pr, axis=1, keepdims=True)
                keep = (row16 <= s_loc) if reverse else (row16 >= s_loc)
                oj = oj + jnp.where(keep, a, 0.0) * v[r:r + 1]
            diag.append(oj)
        o = o + jnp.concatenate(diag, axis=0)
        o_ref[:, sl] = o

        kend = (k * jnp.exp(b_last - b)).astype(BF16)
        st_ref[h] = st * jnp.exp(b_last) + lax.dot_general(vb, kend, TN_DIMS,
                                                           preferred_element_type=F32)


def _hgrn_scan(proj, lbp, *, nh, gq, gf, gi, reverse):
    t = proj.shape[0]
    w = nh * HEAD_DIM
    nc = t // CHUNK
    cidx = (lambda i: nc - 1 - i) if reverse else (lambda i: i)
    spec = lambda g: pl.BlockSpec((CHUNK, w), lambda i: (cidx(i), g))
    return pl.pallas_call(
        functools.partial(_hgrn_body, nh=nh, reverse=reverse),
        grid=(nc,),
        in_specs=[spec(gq), spec(gf), spec(gi),
                  pl.BlockSpec(lbp.shape, lambda i: (0, 0, 0))],
        out_specs=pl.BlockSpec((CHUNK, w), lambda i: (cidx(i), 0)),
        out_shape=jax.ShapeDtypeStruct((t, w), F32),
        scratch_shapes=[pltpu.VMEM((nh, HEAD_DIM, HEAD_DIM), F32)],
        compiler_params=_params(1),
        name="hgrn2_bw" if reverse else "hgrn2_fw",
    )(proj, proj, proj, lbp)


HALO = 8


def _conv_body(x_ref, xp_ref, xn_ref, w_ref, o_ref, xe_ref, *, nh, tt, kw):
    i = pl.program_id(0)
    g = pl.program_id(1)
    n = pl.num_programs(0)
    xe_ref[0:HALO] = jnp.where(i > 0, xp_ref[...], 0.0)
    xe_ref[HALO:HALO + tt] = x_ref[...]
    xe_ref[HALO + tt:HALO + tt + HALO] = jnp.where(i < n - 1, xn_ref[...], 0.0)
    pad = (kw - 1) // 2
    acc = None
    for j in range(kw):
        term = w_ref[j:j + 1] * xe_ref[HALO - pad + j:HALO - pad + j + tt]
        acc = term if acc is None else acc + term
    y = acc * _sigmoid(acc)
    for h in range(nh):
        sl = slice(h * HEAD_DIM, (h + 1) * HEAD_DIM)
        yh = y[:, sl]
        inv = lax.rsqrt(jnp.sum(yh * yh, axis=1, keepdims=True) + EPS)
        scale = jnp.where(g == 0, inv * (HEAD_DIM ** -0.5), jnp.where(g == 1, inv, 1.0))
        o_ref[0, :, sl] = yh * scale


def _conv_qkv(proj, conv_w, *, nh, g0):
    t = proj.shape[0]
    w = nh * HEAD_DIM
    kw = conv_w.shape[0]
    tt = _pick(t, (256, 128, 64))
    nb = tt // HALO
    last = t // HALO - 1
    return pl.pallas_call(
        functools.partial(_conv_body, nh=nh, tt=tt, kw=kw),
        grid=(t // tt, 3),
        in_specs=[pl.BlockSpec((tt, w), lambda i, g: (i, g0 + g)),
                  pl.BlockSpec((HALO, w), lambda i, g: (jnp.maximum(i * nb - 1, 0), g0 + g)),
                  pl.BlockSpec((HALO, w), lambda i, g: (jnp.minimum((i + 1) * nb, last), g0 + g)),
                  pl.BlockSpec((kw, w), lambda i, g: (0, g))],
        out_specs=pl.BlockSpec((1, tt, w), lambda i, g: (g, i, 0)),
        out_shape=jax.ShapeDtypeStruct((3, t, w), F32),
        scratch_shapes=[pltpu.VMEM((tt + 2 * HALO, w), F32)],
        compiler_params=_params(2),
        name="gdn_conv",
    )(proj, proj, proj, conv_w)


def _gdn_body(q_ref, k_ref, v_ref, gab_ref, arow_ref, dtrow_ref, o_ref, st_ref, *, nh, reverse):
    @pl.when(pl.program_id(0) == 0)
    def _():
        st_ref[...] = jnp.zeros_like(st_ref)

    gab = gab_ref[...]
    xs = gab + dtrow_ref[...]
    softplus = jnp.maximum(xs, 0.0) + jnp.log(1.0 + jnp.exp(-jnp.abs(xs)))
    dec = -jnp.exp(arow_ref[...]) * softplus
    beta_all = _sigmoid(gab)
    cum = _tri(reverse).astype(F32)
    gam_all = jnp.dot(cum, dec, precision=HIGHEST, preferred_element_type=F32)
    gam_t = gam_all.T
    incl = _tri(reverse)
    strict = _tri(reverse, strict=True)
    r = lax.broadcasted_iota(jnp.int32, (CHUNK, CHUNK), 0)
    c = lax.broadcasted_iota(jnp.int32, (CHUNK, CHUNK), 1)
    eye = (r == c).astype(F32)
    last = 0 if reverse else CHUNK - 1
    d = 1 if reverse else 0

    for h in range(nh):
        sl = slice(h * HEAD_DIM, (h + 1) * HEAD_DIM)
        ja = d * nh + h
        jb = 2 * nh + d * nh + h
        q = q_ref[0, :, sl]
        k = k_ref[0, :, sl]
        v = v_ref[0, :, sl]
        gam_c = gam_all[:, ja:ja + 1]
        gam_r = gam_t[ja:ja + 1, :]
        beta_c = beta_all[:, jb:jb + 1]
        gam_last = gam_c[last:last + 1]
        rel = jnp.where(incl, jnp.exp(gam_c - gam_r), 0.0)
        qb = q.astype(BF16)
        kb = k.astype(BF16)
        kk = lax.dot_general(kb, kb, NT_DIMS, preferred_element_type=F32)
        m = jnp.where(strict, -(beta_c * kk * rel), 0.0)
        tinv = eye + m
        pw = m
        sq = 1
        while 2 * sq < CHUNK:
            pw = jnp.dot(pw, pw, precision=HIGHEST, preferred_element_type=F32)
            tinv = tinv + jnp.dot(tinv, pw, precision=HIGHEST, preferred_element_type=F32)
            sq *= 2
        rhs = jnp.concatenate([v * beta_c, k * (beta_c * jnp.exp(gam_c))], axis=1)
        sol = jnp.dot(tinv, rhs, precision=HIGHEST, preferred_element_type=F32)
        u = sol[:, :HEAD_DIM]
        wk = sol[:, HEAD_DIM:]
        qk = jnp.where(incl, lax.dot_general(qb, kb, NT_DIMS, preferred_element_type=F32) * rel, 0.0)
        st = st_ref[h]
        stb = st.astype(BF16)
        v_new = u - lax.dot_general(wk.astype(BF16), stb, NT_DIMS, preferred_element_type=F32)
        vnb = v_new.astype(BF16)
        o = (lax.dot_general((q * jnp.exp(gam_c)).astype(BF16), stb, NT_DIMS, preferred_element_type=F32)
             + jnp.dot(qk.astype(BF16), vnb, preferred_element_type=F32))
        o_ref[:, sl] = o
        kdec = (k * jnp.exp(gam_last - gam_c)).astype(BF16)
        st_ref[h] = st * jnp.exp(gam_last) + lax.dot_general(vnb, kdec, TN_DIMS,
                                                             preferred_element_type=F32)


def _gdn_scan(qkv, gab, arow, dtrow, *, nh, reverse):
    t = qkv.shape[1]
    w = nh * HEAD_DIM
    nc = t // CHUNK
    cidx = (lambda i: nc - 1 - i) if reverse else (lambda i: i)
    spec = lambda g: pl.BlockSpec((1, CHUNK, w), lambda i: (g, cidx(i), 0))
    return pl.pallas_call(
        functools.partial(_gdn_body, nh=nh, reverse=reverse),
        grid=(nc,),
        in_specs=[spec(0), spec(1), spec(2),
                  pl.BlockSpec((CHUNK, LANES), lambda i: (cidx(i), 0)),
                  pl.BlockSpec((1, LANES), lambda i: (0, 0)),
                  pl.BlockSpec((1, LANES), lambda i: (0, 0))],
        out_specs=pl.BlockSpec((CHUNK, w), lambda i: (cidx(i), 0)),
        out_shape=jax.ShapeDtypeStruct((t, w), F32),
        scratch_shapes=[pltpu.VMEM((nh, HEAD_DIM, HEAD_DIM), F32)],
        compiler_params=_params(1),
        name="gdn_bw" if reverse else "gdn_fw",
    )(qkv, qkv, qkv, gab, arow, dtrow)


def _gnorm_body(af_ref, ab_ref, bf_ref, bb_ref, hz_ref, gz_ref, wa_ref, wb_ref, o_ref, *, nh):
    w = nh * HEAD_DIM
    for grp, (f_ref, b_ref, z_ref, w_ref) in enumerate(((af_ref, ab_ref, hz_ref, wa_ref),
                                                        (bf_ref, bb_ref, gz_ref, wb_ref))):
        for h in range(nh):
            sl = slice(h * HEAD_DIM, (h + 1) * HEAD_DIM)
            o = f_ref[:, sl] + b_ref[:, sl]
            y = o * lax.rsqrt(jnp.mean(o * o, axis=1, keepdims=True) + EPS) * w_ref[...]
            z = z_ref[:, sl]
            osl = slice(grp * w + h * HEAD_DIM, grp * w + (h + 1) * HEAD_DIM)
            o_ref[:, osl] = (y * (z * _sigmoid(z))).astype(o_ref.dtype)


def _gated_norm(oa_f, oa_b, ob_f, ob_b, proj, wa, wb, *, nh, ghz, ggz):
    t = oa_f.shape[0]
    w = nh * HEAD_DIM
    tm = _pick(t, (256, 128, 64))
    spec = pl.BlockSpec((tm, w), lambda i: (i, 0))
    return pl.pallas_call(
        functools.partial(_gnorm_body, nh=nh),
        grid=(t // tm,),
        in_specs=[spec, spec, spec, spec,
                  pl.BlockSpec((tm, w), lambda i: (i, ghz)),
                  pl.BlockSpec((tm, w), lambda i: (i, ggz)),
                  pl.BlockSpec((1, HEAD_DIM), lambda i: (0, 0)),
                  pl.BlockSpec((1, HEAD_DIM), lambda i: (0, 0))],
        out_specs=pl.BlockSpec((tm, 2 * w), lambda i: (i, 0)),
        out_shape=jax.ShapeDtypeStruct((t, 2 * w), BF16),
        compiler_params=_params(1),
        name="gated_head_norm",
    )(oa_f, oa_b, ob_f, ob_b, proj, proj, wa.reshape(1, HEAD_DIM), wb.reshape(1, HEAD_DIM))


def _cand_rows():
    out = []
    for a in range(PEER_TOPK):
        out.append((a, min(PEER_TOPK, PEER_TOPK // (a + 1))))
    return out


N_CAND = sum(nb for _, nb in _cand_rows())
N_CAND_PAD = -(-N_CAND // 8) * 8


def _route_body(qry_ref, sk_ref, s1_ref, e1_ref, s2_ref, e2_ref, thr_ref, top_ref, cand_ref, *, nheads):
    ninf = -jnp.inf
    for h in range(nheads):
        scores = []
        for p in range(2):
            j = 2 * h + p
            qhp = qry_ref[:, j * PEER_NKEYS:(j + 1) * PEER_NKEYS].astype(BF16)
            keys = sk_ref[h, p].astype(BF16)
            s = lax.dot_general(keys, qhp, NT_DIMS, preferred_element_type=F32)
            scores.append(s)
            wv = s
            for rnk in range(PEER_TOPK):
                mx = jnp.max(wv, axis=0, keepdims=True)
                top_ref[p * PEER_TOPK + rnk:p * PEER_TOPK + rnk + 1, :] = mx
                wv = jnp.where(wv == mx, ninf, wv)
        t1 = top_ref[0:PEER_TOPK]
        t2 = top_ref[PEER_TOPK:2 * PEER_TOPK]
        cand_ref[...] = jnp.full(cand_ref.shape, ninf, F32)
        off = 0
        for a, nb in _cand_rows():
            cand_ref[off:off + nb] = t1[a:a + 1] + t2[0:nb]
            off += nb
        cnd = cand_ref[...]
        cmax = t1[0:1] + t2[0:1]
        z = jnp.zeros_like(cmax)
        mx = cmax
        for rnk in range(PEER_TOPK):
            mx = jnp.max(cnd, axis=0, keepdims=True)
            z = z + jnp.exp(mx - cmax)
            cnd = jnp.where(cnd == mx, ninf, cnd)
        s1, s2 = scores
        s1_ref[h] = s1
        e1_ref[h] = jnp.exp(s1 - t1[0:1]) / z
        s2_ref[h] = s2
        e2_ref[h] = jnp.exp(s2 - t2[0:1])
        thr_ref[h] = mx


def _peer_route(qry, sub_keys):
    t = qry.shape[0]
    nheads = sub_keys.shape[0]
    tt = _pick(t, (512, 256, 128))
    big = pl.BlockSpec((nheads, PEER_NKEYS, tt), lambda i: (0, 0, i))
    big_shape = jax.ShapeDtypeStruct((nheads, PEER_NKEYS, t), F32)
    return pl.pallas_call(
        functools.partial(_route_body, nheads=nheads),
        grid=(t // tt,),
        in_specs=[pl.BlockSpec((tt, qry.shape[1]), lambda i: (i, 0)),
                  pl.BlockSpec(sub_keys.shape, lambda i: (0, 0, 0, 0))],
        out_specs=[big, big, big, big, pl.BlockSpec((nheads, 1, tt), lambda i: (0, 0, i))],
        out_shape=[big_shape, big_shape, big_shape, big_shape,
                   jax.ShapeDtypeStruct((nheads, 1, t), F32)],
        scratch_shapes=[pltpu.VMEM((2 * PEER_TOPK, tt), F32), pltpu.VMEM((N_CAND_PAD, tt), F32)],
        compiler_params=_params(1),
        name="peer_route",
    )(qry, sub_keys)


def _peer_body(xn_ref, u_ref, v_ref, s1_ref, e1_ref, s2_ref, e2_ref, thr_ref, o_ref, g_ref, *, nheads, te):
    e = pl.program_id(1)

    @pl.when(e == 0)
    def _():
        o_ref[...] = jnp.zeros_like(o_ref)

    nrow = te // PEER_NKEYS
    for ii in range(nrow):
        i1 = e * nrow + ii
        acc = None
        for h in range(nheads):
            s1row = s1_ref[h, pl.ds(i1, 1), :]
            e1row = e1_ref[h, pl.ds(i1, 1), :]
            hit = (s2_ref[h] + s1row) >= thr_ref[h]
            term = jnp.where(hit, e2_ref[h] * e1row, 0.0)
            acc = term if acc is None else acc + term
        g_ref[ii * PEER_NKEYS:(ii + 1) * PEER_NKEYS, :] = acc
    hid = lax.dot_general(u_ref[...], xn_ref[...], NT_DIMS, preferred_element_type=F32)
    act = (0.5 * hid * (1.0 + lax.erf(hid * INV_SQRT2)) * g_ref[...]).astype(BF16)
    o_ref[...] += lax.dot_general(act, v_ref[...], TN_DIMS, preferred_element_type=F32)


def _peer_dense(xn, u, v, s1, e1, s2, e2, thr):
    t, d = xn.shape
    ne = u.shape[0]
    nheads = s1.shape[0]
    tt = _pick(t, (512, 256, 128))
    te = _pick(ne, (256, 128))
    big = pl.BlockSpec((nheads, PEER_NKEYS, tt), lambda i, e: (0, 0, i))
    return pl.pallas_call(
        functools.partial(_peer_body, nheads=nheads, te=te),
        grid=(t // tt, ne // te),
        in_specs=[pl.BlockSpec((tt, d), lambda i, e: (i, 0)),
                  pl.BlockSpec((te, d), lambda i, e: (e, 0)),
                  pl.BlockSpec((te, d), lambda i, e: (e, 0)),
                  big, big, big, big,
                  pl.BlockSpec((nheads, 1, tt), lambda i, e: (0, 0, i))],
        out_specs=pl.BlockSpec((tt, d), lambda i, e: (i, 0)),
        out_shape=jax.ShapeDtypeStruct((t, d), F32),
        scratch_shapes=[pltpu.VMEM((te, tt), F32)],
        compiler_params=_params(2),
        name="peer_dense",
    )(xn, u, v, s1, e1, s2, e2, thr)


def _add_rms_body(a_ref, b_ref, w_ref, s_ref, n_ref):
    x = a_ref[...] + b_ref[...]
    s_ref[...] = x
    ms = jnp.mean(x * x, axis=-1, keepdims=True)
    n_ref[...] = (x * lax.rsqrt(ms + EPS) * w_ref[...]).astype(n_ref.dtype)


def _add_rmsnorm(a, b, w):
    t, d = a.shape
    tm = _pick(t, (256, 128, 64, 8))
    spec = pl.BlockSpec((tm, d), lambda i: (i, 0))
    return pl.pallas_call(
        _add_rms_body,
        grid=(t // tm,),
        in_specs=[spec, spec, pl.BlockSpec((1, d), lambda i: (0, 0))],
        out_specs=[spec, spec],
        out_shape=[jax.ShapeDtypeStruct((t, d), F32), jax.ShapeDtypeStruct((t, d), BF16)],
        compiler_params=_params(1),
        name="add_rmsnorm",
    )(a, b, w.reshape(1, d))


def _ple_body(xn_ref, wg_ref, p_ref, wp_ref, h_ref, pw_ref, fw_ref, o_ref, emb_ref, *, tn):
    j = pl.program_id(1)

    @pl.when(j == 0)
    def _():
        raw = jnp.dot(p_ref[...], wp_ref[...], preferred_element_type=F32)
        ms = jnp.mean(raw * raw, axis=-1, keepdims=True)
        emb_ref[...] = raw * lax.rsqrt(ms + EPS) * pw_ref[...]

    cs = pl.ds(pl.multiple_of(j * tn, tn), tn)
    gate = _sigmoid(jnp.dot(xn_ref[...], wg_ref[...], preferred_element_type=F32))
    o_ref[:, cs] = h_ref[:, cs] + gate * emb_ref[:, cs]

    @pl.when(j == pl.num_programs(1) - 1)
    def _():
        x = o_ref[...]
        ms = jnp.mean(x * x, axis=-1, keepdims=True)
        o_ref[...] = x * lax.rsqrt(ms + EPS) * fw_ref[...]


def _ple_final(xn, w_gate, p, w_proj, h, post_w, final_w):
    t, d = h.shape
    pd = p.shape[1]
    tm = _pick(t, (256, 128, 64, 8))
    tn = _pick(d, (512, 256, 128))
    return pl.pallas_call(
        functools.partial(_ple_body, tn=tn),
        grid=(t // tm, d // tn),
        in_specs=[pl.BlockSpec((tm, d), lambda i, j: (i, 0)),
                  pl.BlockSpec((d, tn), lambda i, j: (0, j)),
                  pl.BlockSpec((tm, pd), lambda i, j: (i, 0)),
                  pl.BlockSpec((pd, d), lambda i, j: (0, 0)),
                  pl.BlockSpec((tm, d), lambda i, j: (i, 0)),
                  pl.BlockSpec((1, d), lambda i, j: (0, 0)),
                  pl.BlockSpec((1, d), lambda i, j: (0, 0))],
        out_specs=pl.BlockSpec((tm, d), lambda i, j: (i, 0)),
        out_shape=jax.ShapeDtypeStruct((t, d), F32),
        scratch_shapes=[pltpu.VMEM((tm, d), F32)],
        compiler_params=_params(2),
        name="ple_final",
    )(xn, w_gate, p, w_proj, h, post_w.reshape(1, d), final_w.reshape(1, d))


def kernel(x, p, attn_norm_w, w_in, hg_lower_bound, gd_conv_w, gd_A_log, gd_dt_bias, hg_out_norm_w, gd_out_norm_w, w_out, ffn_norm_w, peer_w_query, peer_sub_keys, peer_u, peer_v, ple_norm_w, ple_w_gate, ple_w_proj, ple_post_norm_w, final_norm_w):
    bsz, seq, d = x.shape
    depth = w_in.shape[0]
    assert bsz == 1 and depth == 1
    w = d // 2
    nh = w // HEAD_DIM
    assert 4 * nh <= LANES
    t = bsz * seq
    h = x.reshape(t, d)
    i = 0

    wi = w_in[i]
    n_main = 8 * w
    w_main = jnp.concatenate([wi[:, :n_main], wi[:, n_main + 4 * nh:]], axis=1).astype(BF16)
    w_gate_cols = jnp.concatenate([wi[:, n_main:n_main + 4 * nh],
                                   jnp.zeros((d, LANES - 4 * nh), wi.dtype)], axis=1).astype(BF16)
    xn = _rmsnorm(h, attn_norm_w[i])
    proj = _matmul(xn, w_main, name="in_proj")
    gab = _matmul(xn, w_gate_cols, name="in_proj_gates")

    oa = []
    for dr, rev in ((0, False), (1, True)):
        lbp = hg_lower_bound[:, dr, :].reshape(depth + 1, 1, w)
        oa.append(_hgrn_scan(proj, lbp, nh=nh, gq=0, gf=1 + dr, gi=3, reverse=rev))

    qkv = _conv_qkv(proj, gd_conv_w[i], nh=nh, g0=5)
    pad = jnp.zeros((LANES - 2 * nh,), F32)
    arow = jnp.concatenate([gd_A_log[i].reshape(-1), pad]).reshape(1, LANES)
    dtrow = jnp.concatenate([gd_dt_bias[i].reshape(-1), pad]).reshape(1, LANES)
    ob = [_gdn_scan(qkv, gab, arow, dtrow, nh=nh, reverse=rev) for rev in (False, True)]

    mix_in = _gated_norm(oa[0], oa[1], ob[0], ob[1], proj, hg_out_norm_w[i], gd_out_norm_w[i],
                         nh=nh, ghz=4, ggz=8)
    h1 = _matmul(mix_in, w_out[i].astype(BF16), residual=h, name="out_proj")

    xn2 = _rmsnorm(h1, ffn_norm_w[i])
    qry = _matmul(xn2, peer_w_query[i].astype(BF16), name="peer_query")
    s1, e1, s2, e2, thr = _peer_route(qry, peer_sub_keys[i])
    peer_out = _peer_dense(xn2, peer_u[i].astype(BF16), peer_v[i].astype(BF16), s1, e1, s2, e2, thr)
    h2, xn3 = _add_rmsnorm(h1, peer_out, ple_norm_w[i])

    out = _ple_final(xn3, ple_w_gate[i].astype(BF16), p[i].reshape(t, -1).astype(BF16),
                     ple_w_proj[i].astype(BF16), h2, ple_post_norm_w[i], final_norm_w)
    return out.reshape(bsz, seq, d)
```

```python
import functools

import jax
import jax.numpy as jnp
from jax import lax
from jax.experimental import pallas as pl
from jax.experimental.pallas import tpu as pltpu

F32 = jnp.float32
BF16 = jnp.bfloat16
EPS = 1e-6
HEAD_DIM = 128
CHUNK = 64
SUB = 16
PEER_TOPK = 16
PEER_NKEYS = 128
LANES = 128
VMEM_LIMIT_BYTES = 56 * 1024 * 1024
HIGHEST = lax.Precision.HIGHEST
NT_DIMS = (((1,), (1,)), ((), ()))
TN_DIMS = (((0,), (0,)), ((), ()))
INV_SQRT2 = 0.7071067811865476


def _params(n_axes):
    return pltpu.CompilerParams(dimension_semantics=("arbitrary",) * n_axes,
                                vmem_limit_bytes=VMEM_LIMIT_BYTES)


def _sigmoid(x):
    return 1.0 / (1.0 + jnp.exp(-x))


def _pick(n, prefs):
    for t in prefs:
        if n % t == 0:
            return t
    return n


def _rms_body(x_ref, w_ref, o_ref):
    x = x_ref[...]
    ms = jnp.mean(x * x, axis=-1, keepdims=True)
    o_ref[...] = (x * lax.rsqrt(ms + EPS) * w_ref[...]).astype(o_ref.dtype)


def _rmsnorm(x, w):
    t, d = x.shape
    tm = _pick(t, (512, 256, 128, 64, 8))
    return pl.pallas_call(
        _rms_body,
        grid=(t // tm,),
        in_specs=[pl.BlockSpec((tm, d), lambda i: (i, 0)),
                  pl.BlockSpec((1, d), lambda i: (0, 0))],
        out_specs=pl.BlockSpec((tm, d), lambda i: (i, 0)),
        out_shape=jax.ShapeDtypeStruct((t, d), BF16),
        compiler_params=_params(1),
        name="rmsnorm",
    )(x, w.reshape(1, d))


def _mm_body(a_ref, b_ref, o_ref):
    o_ref[...] = jnp.dot(a_ref[...], b_ref[...], preferred_element_type=F32)


def _mm_res_body(a_ref, b_ref, r_ref, o_ref):
    o_ref[...] = r_ref[...] + jnp.dot(a_ref[...], b_ref[...], preferred_element_type=F32)


def _matmul(a, b, residual=None, name="matmul"):
    m, k = a.shape
    n = b.shape[1]
    tm = _pick(m, (1024, 512, 256, 128, 64, 8))
    tn = _pick(n, (512, 256, 128))
    in_specs = [pl.BlockSpec((tm, k), lambda i, j: (i, 0)),
                pl.BlockSpec((k, tn), lambda i, j: (0, j))]
    args = [a, b]
    body = _mm_body
    if residual is not None:
        in_specs.append(pl.BlockSpec((tm, tn), lambda i, j: (i, j)))
        args.append(residual)
        body = _mm_res_body
    return pl.pallas_call(
        body,
        grid=(m // tm, n // tn),
        in_specs=in_specs,
        out_specs=pl.BlockSpec((tm, tn), lambda i, j: (i, j)),
        out_shape=jax.ShapeDtypeStruct((m, n), F32),
        compiler_params=_params(2),
        name=name,
    )(*args)


def _tri(reverse, strict=False):
    r = lax.broadcasted_iota(jnp.int32, (CHUNK, CHUNK), 0)
    c = lax.broadcasted_iota(jnp.int32, (CHUNK, CHUNK), 1)
    if reverse:
        return (c > r) if strict else (c >= r)
    return (c < r) if strict else (c <= r)


def _hgrn_body(hq_ref, hf_ref, hi_ref, lbp_ref, o_ref, st_ref, *, nh, reverse):
    @pl.when(pl.program_id(0) == 0)
    def _():
        st_ref[...] = jnp.zeros_like(st_ref)

    nl = lbp_ref.shape[0]
    mx = lbp_ref[0]
    for l in range(1, nl):
        mx = jnp.maximum(mx, lbp_ref[l])
    den = jnp.exp(lbp_ref[0] - mx)
    e0 = den
    for l in range(1, nl):
        den = den + jnp.exp(lbp_ref[l] - mx)
    lb = e0 / den

    cum = _tri(reverse).astype(F32)
    col = lax.broadcasted_iota(jnp.int32, (SUB, CHUNK), 1)
    row16 = lax.broadcasted_iota(jnp.int32, (SUB, 1), 0)
    nsub = CHUNK // SUB
    last = 0 if reverse else CHUNK - 1

    for h in range(nh):
        sl = slice(h * HEAD_DIM, (h + 1) * HEAD_DIM)
        zq = hq_ref[:, sl]
        zf = hf_ref[:, sl]
        v = hi_ref[:, sl]
        lbh = lb[:, sl]
        q = zq * _sigmoid(zq) * (HEAD_DIM ** -0.5)
        s = _sigmoid(zf)
        logf = jnp.log(lbh + (1.0 - lbh) * s)
        k = (1.0 - lbh) * (1.0 - s)
        b = jnp.dot(cum, logf, precision=HIGHEST, preferred_element_type=F32)
        b_last = b[last:last + 1]
        vb = v.astype(BF16)
        st = st_ref[h]
        o = lax.dot_general((q * jnp.exp(b)).astype(BF16), st.astype(BF16), NT_DIMS,
                            preferred_element_type=F32)

        rows = []
        for i in range(nsub):
            first = (i == nsub - 1) if reverse else (i == 0)
            if first:
                rows.append(jnp.zeros((SUB, CHUNK), F32))
                continue
            rb = SUB * (i + 1) if reverse else SUB * i - 1
            ref = b[rb:rb + 1]
            qt = (q[SUB * i:SUB * (i + 1)] * jnp.exp(b[SUB * i:SUB * (i + 1)] - ref)).astype(BF16)
            kh = (k * jnp.exp(jnp.minimum(ref - b, 0.0))).astype(BF16)
            a = lax.dot_general(qt, kh, NT_DIMS, preferred_element_type=F32)
            keep = (col >= SUB * (i + 1)) if reverse else (col < SUB * i)
            rows.append(jnp.where(keep, a, 0.0))
        a_cross = jnp.concatenate(rows, axis=0)
        o = o + jnp.dot(a_cross.astype(BF16), vb, preferred_element_type=F32)

        diag = []
        for j in range(nsub):
            qj = q[SUB * j:SUB * (j + 1)]
            bj = b[SUB * j:SUB * (j + 1)]
            oj = jnp.zeros((SUB, HEAD_DIM), F32)
            for s_loc in range(SUB):
                r = SUB * j + s_loc
                pr = qj * k[r:r + 1] * jnp.exp(bj - b[r:r + 1])
                a = jnp.sum(pr, axis=1, keepdims=True)
                keep = (row16 <= s_loc) if reverse else (row16 >= s_loc)
                oj = oj + jnp.where(keep, a, 0.0) * v[r:r + 1]
            diag.append(oj)
        o = o + jnp.concatenate(diag, axis=0)
        o_ref[:, sl] = o

        kend = (k * jnp.exp(b_last - b)).astype(BF16)
        st_ref[h] = st * jnp.exp(b_last) + lax.dot_general(vb, kend, TN_DIMS,
                                                           preferred_element_type=F32)


def _hgrn_scan(proj, lbp, *, nh, gq, gf, gi, reverse):
    t = proj.shape[0]
    w = nh * HEAD_DIM
    nc = t // CHUNK
    cidx = (lambda i: nc - 1 - i) if reverse else (lambda i: i)
    spec = lambda g: pl.BlockSpec((CHUNK, w), lambda i: (cidx(i), g))
    return pl.pallas_call(
        functools.partial(_hgrn_body, nh=nh, reverse=reverse),
        grid=(nc,),
        in_specs=[spec(gq), spec(gf), spec(gi),
                  pl.BlockSpec(lbp.shape, lambda i: (0, 0, 0))],
        out_specs=pl.BlockSpec((CHUNK, w), lambda i: (cidx(i), 0)),
        out_shape=jax.ShapeDtypeStruct((t, w), F32),
        scratch_shapes=[pltpu.VMEM((nh, HEAD_DIM, HEAD_DIM), F32)],
        compiler_params=_params(1),
        name="hgrn2_bw" if reverse else "hgrn2_fw",
    )(proj, proj, proj, lbp)


HALO = 8


def _conv_body(x_ref, xp_ref, xn_ref, w_ref, o_ref, xe_ref, *, nh, tt, kw):
    i = pl.program_id(0)
    g = pl.program_id(1)
    n = pl.num_programs(0)
    xe_ref[0:HALO] = jnp.where(i > 0, xp_ref[...], 0.0)
    xe_ref[HALO:HALO + tt] = x_ref[...]
    xe_ref[HALO + tt:HALO + tt + HALO] = jnp.where(i < n - 1, xn_ref[...], 0.0)
    pad = (kw - 1) // 2
    acc = None
    for j in range(kw):
        term = w_ref[j:j + 1] * xe_ref[HALO - pad + j:HALO - pad + j + tt]
        acc = term if acc is None else acc + term
    y = acc * _sigmoid(acc)
    for h in range(nh):
        sl = slice(h * HEAD_DIM, (h + 1) * HEAD_DIM)
        yh = y[:, sl]
        inv = lax.rsqrt(jnp.sum(yh * yh, axis=1, keepdims=True) + EPS)
        scale = jnp.where(g == 0, inv * (HEAD_DIM ** -0.5), jnp.where(g == 1, inv, 1.0))
        o_ref[0, :, sl] = yh * scale


def _conv_qkv(proj, conv_w, *, nh, g0):
    t = proj.shape[0]
    w = nh * HEAD_DIM
    kw = conv_w.shape[0]
    tt = _pick(t, (256, 128, 64))
    nb = tt // HALO
    last = t // HALO - 1
    return pl.pallas_call(
        functools.partial(_conv_body, nh=nh, tt=tt, kw=kw),
        grid=(t // tt, 3),
        in_specs=[pl.BlockSpec((tt, w), lambda i, g: (i, g0 + g)),
                  pl.BlockSpec((HALO, w), lambda i, g: (jnp.maximum(i * nb - 1, 0), g0 + g)),
                  pl.BlockSpec((HALO, w), lambda i, g: (jnp.minimum((i + 1) * nb, last), g0 + g)),
                  pl.BlockSpec((kw, w), lambda i, g: (0, g))],
        out_specs=pl.BlockSpec((1, tt, w), lambda i, g: (g, i, 0)),
        out_shape=jax.ShapeDtypeStruct((3, t, w), F32),
        scratch_shapes=[pltpu.VMEM((tt + 2 * HALO, w), F32)],
        compiler_params=_params(2),
        name="gdn_conv",
    )(proj, proj, proj, conv_w)


def _gdn_body(q_ref, k_ref, v_ref, gab_ref, arow_ref, dtrow_ref, o_ref, st_ref, *, nh, reverse):
    @pl.when(pl.program_id(0) == 0)
    def _():
        st_ref[...] = jnp.zeros_like(st_ref)

    gab = gab_ref[...]
    xs = gab + dtrow_ref[...]
    softplus = jnp.maximum(xs, 0.0) + jnp.log(1.0 + jnp.exp(-jnp.abs(xs)))
    dec = -jnp.exp(arow_ref[...]) * softplus
    beta_all = _sigmoid(gab)
    cum = _tri(reverse).astype(F32)
    gam_all = jnp.dot(cum, dec, precision=HIGHEST, preferred_element_type=F32)
    gam_t = gam_all.T
    incl = _tri(reverse)
    strict = _tri(reverse, strict=True)
    r = lax.broadcasted_iota(jnp.int32, (CHUNK, CHUNK), 0)
    c = lax.broadcasted_iota(jnp.int32, (CHUNK, CHUNK), 1)
    eye = (r == c).astype(F32)
    last = 0 if reverse else CHUNK - 1
    d = 1 if reverse else 0

    heads = range(nh)
    sls = [slice(h * HEAD_DIM, (h + 1) * HEAD_DIM) for h in heads]
    gam_c = [gam_all[:, d * nh + h:d * nh + h + 1] for h in heads]
    beta_c = [beta_all[:, 2 * nh + d * nh + h:2 * nh + d * nh + h + 1] for h in heads]
    rel = [jnp.where(incl, jnp.exp(gam_c[h] - gam_t[d * nh + h:d * nh + h + 1, :]), 0.0) for h in heads]
    kb = [k_ref[0, :, sls[h]].astype(BF16) for h in heads]
    kk = [lax.dot_general(kb[h], kb[h], NT_DIMS, preferred_element_type=F32) for h in heads]
    qk = [lax.dot_general(q_ref[0, :, sls[h]].astype(BF16), kb[h], NT_DIMS, preferred_element_type=F32)
          for h in heads]
    qk = [jnp.where(incl, qk[h] * rel[h], 0.0).astype(BF16) for h in heads]
    m = [jnp.where(strict, -(beta_c[h] * kk[h] * rel[h]), 0.0) for h in heads]
    tinv = [eye + m[h] for h in heads]
    pw = [m[h].astype(BF16) for h in heads]
    sq = 1
    while 2 * sq < CHUNK:
        pw = [jnp.dot(pw[h], pw[h], preferred_element_type=F32).astype(BF16) for h in heads]
        tinv = [tinv[h] + jnp.dot(tinv[h].astype(BF16), pw[h], preferred_element_type=F32) for h in heads]
        sq *= 2
    sol = []
    for h in heads:
        k = k_ref[0, :, sls[h]]
        rhs = jnp.concatenate([v_ref[0, :, sls[h]] * beta_c[h], k * (beta_c[h] * jnp.exp(gam_c[h]))], axis=1)
        sol.append(jnp.dot(tinv[h].astype(BF16), rhs.astype(BF16), preferred_element_type=F32))
    stb = [st_ref[h].astype(BF16) for h in heads]
    vnb = [(sol[h][:, :HEAD_DIM]
            - lax.dot_general(sol[h][:, HEAD_DIM:].astype(BF16), stb[h], NT_DIMS, preferred_element_type=F32)
            ).astype(BF16) for h in heads]
    for h in heads:
        qdec = (q_ref[0, :, sls[h]] * jnp.exp(gam_c[h])).astype(BF16)
        o_ref[:, sls[h]] = (lax.dot_general(qdec, stb[h], NT_DIMS, preferred_element_type=F32)
                            + jnp.dot(qk[h], vnb[h], preferred_element_type=F32))
    for h in heads:
        gam_last = gam_c[h][last:last + 1]
        kdec = (k_ref[0, :, sls[h]] * jnp.exp(gam_last - gam_c[h])).astype(BF16)
        st_ref[h] = st_ref[h] * jnp.exp(gam_last) + lax.dot_general(vnb[h], kdec, TN_DIMS,
                                                                    preferred_element_type=F32)


def _gdn_scan(qkv, gab, arow, dtrow, *, nh, reverse):
    t = qkv.shape[1]
    w = nh * HEAD_DIM
    nc = t // CHUNK
    cidx = (lambda i: nc - 1 - i) if reverse else (lambda i: i)
    spec = lambda g: pl.BlockSpec((1, CHUNK, w), lambda i: (g, cidx(i), 0))
    return pl.pallas_call(
        functools.partial(_gdn_body, nh=nh, reverse=reverse),
        grid=(nc,),
        in_specs=[spec(0), spec(1), spec(2),
                  pl.BlockSpec((CHUNK, LANES), lambda i: (cidx(i), 0)),
                  pl.BlockSpec((1, LANES), lambda i: (0, 0)),
                  pl.BlockSpec((1, LANES), lambda i: (0, 0))],
        out_specs=pl.BlockSpec((CHUNK, w), lambda i: (cidx(i), 0)),
        out_shape=jax.ShapeDtypeStruct((t, w), F32),
        scratch_shapes=[pltpu.VMEM((nh, HEAD_DIM, HEAD_DIM), F32)],
        compiler_params=_params(1),
        name="gdn_bw" if reverse else "gdn_fw",
    )(qkv, qkv, qkv, gab, arow, dtrow)


def _gnorm_body(af_ref, ab_ref, bf_ref, bb_ref, hz_ref, gz_ref, wa_ref, wb_ref, o_ref, *, nh):
    w = nh * HEAD_DIM
    for grp, (f_ref, b_ref, z_ref, w_ref) in enumerate(((af_ref, ab_ref, hz_ref, wa_ref),
                                                        (bf_ref, bb_ref, gz_ref, wb_ref))):
        for h in range(nh):
            sl = slice(h * HEAD_DIM, (h + 1) * HEAD_DIM)
            o = f_ref[:, sl] + b_ref[:, sl]
            y = o * lax.rsqrt(jnp.mean(o * o, axis=1, keepdims=True) + EPS) * w_ref[...]
            z = z_ref[:, sl]
            osl = slice(grp * w + h * HEAD_DIM, grp * w + (h + 1) * HEAD_DIM)
            o_ref[:, osl] = (y * (z * _sigmoid(z))).astype(o_ref.dtype)


def _gated_norm(oa_f, oa_b, ob_f, ob_b, proj, wa, wb, *, nh, ghz, ggz):
    t = oa_f.shape[0]
    w = nh * HEAD_DIM
    tm = _pick(t, (256, 128, 64))
    spec = pl.BlockSpec((tm, w), lambda i: (i, 0))
    return pl.pallas_call(
        functools.partial(_gnorm_body, nh=nh),
        grid=(t // tm,),
        in_specs=[spec, spec, spec, spec,
                  pl.BlockSpec((tm, w), lambda i: (i, ghz)),
                  pl.BlockSpec((tm, w), lambda i: (i, ggz)),
                  pl.BlockSpec((1, HEAD_DIM), lambda i: (0, 0)),
                  pl.BlockSpec((1, HEAD_DIM), lambda i: (0, 0))],
        out_specs=pl.BlockSpec((tm, 2 * w), lambda i: (i, 0)),
        out_shape=jax.ShapeDtypeStruct((t, 2 * w), BF16),
        compiler_params=_params(1),
        name="gated_head_norm",
    )(oa_f, oa_b, ob_f, ob_b, proj, proj, wa.reshape(1, HEAD_DIM), wb.reshape(1, HEAD_DIM))


def _cand_rows():
    out = []
    for a in range(PEER_TOPK):
        out.append((a, min(PEER_TOPK, PEER_TOPK // (a + 1))))
    return out


N_CAND = sum(nb for _, nb in _cand_rows())
N_CAND_PAD = -(-N_CAND // 8) * 8


def _route_body(qry_ref, sk_ref, s1_ref, e1_ref, s2_ref, e2_ref, thr_ref, top_ref, cand_ref, *, nheads):
    ninf = -jnp.inf
    for h in range(nheads):
        scores = []
        for p in range(2):
            j = 2 * h + p
            qhp = qry_ref[:, j * PEER_NKEYS:(j + 1) * PEER_NKEYS].astype(BF16)
            keys = sk_ref[h, p].astype(BF16)
            s = lax.dot_general(keys, qhp, NT_DIMS, preferred_element_type=F32)
            scores.append(s)
            wv = s
            for rnk in range(PEER_TOPK):
                mx = jnp.max(wv, axis=0, keepdims=True)
                top_ref[p * PEER_TOPK + rnk:p * PEER_TOPK + rnk + 1, :] = mx
                wv = jnp.where(wv == mx, ninf, wv)
        t1 = top_ref[0:PEER_TOPK]
        t2 = top_ref[PEER_TOPK:2 * PEER_TOPK]
        cand_ref[...] = jnp.full(cand_ref.shape, ninf, F32)
        off = 0
        for a, nb in _cand_rows():
            cand_ref[off:off + nb] = t1[a:a + 1] + t2[0:nb]
            off += nb
        cnd = cand_ref[...]
        cmax = t1[0:1] + t2[0:1]
        z = jnp.zeros_like(cmax)
        mx = cmax
        for rnk in range(PEER_TOPK):
            mx = jnp.max(cnd, axis=0, keepdims=True)
            z = z + jnp.exp(mx - cmax)
            cnd = jnp.where(cnd == mx, ninf, cnd)
        s1, s2 = scores
        s1_ref[h] = s1
        e1_ref[h] = jnp.exp(s1 - t1[0:1]) / z
        s2_ref[h] = s2
        e2_ref[h] = jnp.exp(s2 - t2[0:1])
        thr_ref[h] = mx


def _peer_route(qry, sub_keys):
    t = qry.shape[0]
    nheads = sub_keys.shape[0]
    tt = _pick(t, (512, 256, 128))
    big = pl.BlockSpec((nheads, PEER_NKEYS, tt), lambda i: (0, 0, i))
    big_shape = jax.ShapeDtypeStruct((nheads, PEER_NKEYS, t), F32)
    return pl.pallas_call(
        functools.partial(_route_body, nheads=nheads),
        grid=(t // tt,),
        in_specs=[pl.BlockSpec((tt, qry.shape[1]), lambda i: (i, 0)),
                  pl.BlockSpec(sub_keys.shape, lambda i: (0, 0, 0, 0))],
        out_specs=[big, big, big, big, pl.BlockSpec((nheads, 1, tt), lambda i: (0, 0, i))],
        out_shape=[big_shape, big_shape, big_shape, big_shape,
                   jax.ShapeDtypeStruct((nheads, 1, t), F32)],
        scratch_shapes=[pltpu.VMEM((2 * PEER_TOPK, tt), F32), pltpu.VMEM((N_CAND_PAD, tt), F32)],
        compiler_params=_params(1),
        name="peer_route",
    )(qry, sub_keys)


def _peer_body(xn_ref, u_ref, v_ref, s1_ref, e1_ref, s2_ref, e2_ref, thr_ref, o_ref, g_ref, *, nheads, te):
    @pl.when(pl.program_id(1) == 0)
    def _():
        o_ref[...] = jnp.zeros_like(o_ref)

    nrow = te // PEER_NKEYS
    hid = lax.dot_general(u_ref[...], xn_ref[...], NT_DIMS, preferred_element_type=F32)
    for ii in range(nrow):
        acc = None
        for h in range(nheads):
            s1row = s1_ref[0, h * nrow + ii:h * nrow + ii + 1, :]
            e1row = e1_ref[0, h * nrow + ii:h * nrow + ii + 1, :]
            hit = (s2_ref[h] + s1row) >= thr_ref[h]
            term = jnp.where(hit, e2_ref[h] * e1row, 0.0)
            acc = term if acc is None else acc + term
        g_ref[ii * PEER_NKEYS:(ii + 1) * PEER_NKEYS, :] = acc
    act = (0.5 * hid * (1.0 + lax.erf(hid * INV_SQRT2)) * g_ref[...]).astype(BF16)
    o_ref[...] += lax.dot_general(act, v_ref[...], TN_DIMS, preferred_element_type=F32)


def _peer_dense(xn, u, v, s1, e1, s2, e2, thr):
    t, d = xn.shape
    ne = u.shape[0]
    nheads = s1.shape[0]
    tt = _pick(t, (512, 256, 128))
    te = _pick(ne, (512, 256, 128))
    nrow = te // PEER_NKEYS
    big = pl.BlockSpec((nheads, PEER_NKEYS, tt), lambda i, e: (0, 0, i))
    rows = pl.BlockSpec((1, nheads * nrow, tt), lambda i, e: (e, 0, i))

    def by_block(a):
        return a.reshape(nheads, PEER_NKEYS // nrow, nrow, t).transpose(1, 0, 2, 3).reshape(
            PEER_NKEYS // nrow, nheads * nrow, t)

    s1, e1 = by_block(s1), by_block(e1)
    return pl.pallas_call(
        functools.partial(_peer_body, nheads=nheads, te=te),
        grid=(t // tt, ne // te),
        in_specs=[pl.BlockSpec((tt, d), lambda i, e: (i, 0)),
                  pl.BlockSpec((te, d), lambda i, e: (e, 0)),
                  pl.BlockSpec((te, d), lambda i, e: (e, 0)),
                  rows, rows, big, big,
                  pl.BlockSpec((nheads, 1, tt), lambda i, e: (0, 0, i))],
        out_specs=pl.BlockSpec((tt, d), lambda i, e: (i, 0)),
        out_shape=jax.ShapeDtypeStruct((t, d), F32),
        scratch_shapes=[pltpu.VMEM((te, tt), F32)],
        compiler_params=_params(2),
        name="peer_dense",
    )(xn, u, v, s1, e1, s2, e2, thr)


def _add_rms_body(a_ref, b_ref, w_ref, s_ref, n_ref):
    x = a_ref[...] + b_ref[...]
    s_ref[...] = x
    ms = jnp.mean(x * x, axis=-1, keepdims=True)
    n_ref[...] = (x * lax.rsqrt(ms + EPS) * w_ref[...]).astype(n_ref.dtype)


def _add_rmsnorm(a, b, w):
    t, d = a.shape
    tm = _pick(t, (256, 128, 64, 8))
    spec = pl.BlockSpec((tm, d), lambda i: (i, 0))
    return pl.pallas_call(
        _add_rms_body,
        grid=(t // tm,),
        in_specs=[spec, spec, pl.BlockSpec((1, d), lambda i: (0, 0))],
        out_specs=[spec, spec],
        out_shape=[jax.ShapeDtypeStruct((t, d), F32), jax.ShapeDtypeStruct((t, d), BF16)],
        compiler_params=_params(1),
        name="add_rmsnorm",
    )(a, b, w.reshape(1, d))


def _ple_body(xn_ref, wg_ref, p_ref, wp_ref, h_ref, pw_ref, fw_ref, o_ref, emb_ref, *, tn):
    j = pl.program_id(1)

    @pl.when(j == 0)
    def _():
        raw = jnp.dot(p_ref[...], wp_ref[...], preferred_element_type=F32)
        ms = jnp.mean(raw * raw, axis=-1, keepdims=True)
        emb_ref[...] = raw * lax.rsqrt(ms + EPS) * pw_ref[...]

    cs = pl.ds(pl.multiple_of(j * tn, tn), tn)
    gate = _sigmoid(jnp.dot(xn_ref[...], wg_ref[...], preferred_element_type=F32))
    o_ref[:, cs] = h_ref[:, cs] + gate * emb_ref[:, cs]

    @pl.when(j == pl.num_programs(1) - 1)
    def _():
        x = o_ref[...]
        ms = jnp.mean(x * x, axis=-1, keepdims=True)
        o_ref[...] = x * lax.rsqrt(ms + EPS) * fw_ref[...]


def _ple_final(xn, w_gate, p, w_proj, h, post_w, final_w):
    t, d = h.shape
    pd = p.shape[1]
    tm = _pick(t, (256, 128, 64, 8))
    tn = _pick(d, (512, 256, 128))
    return pl.pallas_call(
        functools.partial(_ple_body, tn=tn),
        grid=(t // tm, d // tn),
        in_specs=[pl.BlockSpec((tm, d), lambda i, j: (i, 0)),
                  pl.BlockSpec((d, tn), lambda i, j: (0, j)),
                  pl.BlockSpec((tm, pd), lambda i, j: (i, 0)),
                  pl.BlockSpec((pd, d), lambda i, j: (0, 0)),
                  pl.BlockSpec((tm, d), lambda i, j: (i, 0)),
                  pl.BlockSpec((1, d), lambda i, j: (0, 0)),
                  pl.BlockSpec((1, d), lambda i, j: (0, 0))],
        out_specs=pl.BlockSpec((tm, d), lambda i, j: (i, 0)),
        out_shape=jax.ShapeDtypeStruct((t, d), F32),
        scratch_shapes=[pltpu.VMEM((tm, d), F32)],
        compiler_params=_params(2),
        name="ple_final",
    )(xn, w_gate, p, w_proj, h, post_w.reshape(1, d), final_w.reshape(1, d))


def kernel(x, p, attn_norm_w, w_in, hg_lower_bound, gd_conv_w, gd_A_log, gd_dt_bias, hg_out_norm_w, gd_out_norm_w, w_out, ffn_norm_w, peer_w_query, peer_sub_keys, peer_u, peer_v, ple_norm_w, ple_w_gate, ple_w_proj, ple_post_norm_w, final_norm_w):
    bsz, seq, d = x.shape
    depth = w_in.shape[0]
    assert bsz == 1 and depth == 1
    w = d // 2
    nh = w // HEAD_DIM
    assert 4 * nh <= LANES
    t = bsz * seq
    h = x.reshape(t, d)
    i = 0

    wi = w_in[i]
    n_main = 8 * w
    w_main = jnp.concatenate([wi[:, :n_main], wi[:, n_main + 4 * nh:]], axis=1).astype(BF16)
    w_gate_cols = jnp.concatenate([wi[:, n_main:n_main + 4 * nh],
                                   jnp.zeros((d, LANES - 4 * nh), wi.dtype)], axis=1).astype(BF16)
    xn = _rmsnorm(h, attn_norm_w[i])
    proj = _matmul(xn, w_main, name="in_proj")
    gab = _matmul(xn, w_gate_cols, name="in_proj_gates")

    oa = []
    for dr, rev in ((0, False), (1, True)):
        lbp = hg_lower_bound[:, dr, :].reshape(depth + 1, 1, w)
        oa.append(_hgrn_scan(proj, lbp, nh=nh, gq=0, gf=1 + dr, gi=3, reverse=rev))

    qkv = _conv_qkv(proj, gd_conv_w[i], nh=nh, g0=5)
    pad = jnp.zeros((LANES - 2 * nh,), F32)
    arow = jnp.concatenate([gd_A_log[i].reshape(-1), pad]).reshape(1, LANES)
    dtrow = jnp.concatenate([gd_dt_bias[i].reshape(-1), pad]).reshape(1, LANES)
    ob = [_gdn_scan(qkv, gab, arow, dtrow, nh=nh, reverse=rev) for rev in (False, True)]

    mix_in = _gated_norm(oa[0], oa[1], ob[0], ob[1], proj, hg_out_norm_w[i], gd_out_norm_w[i],
                         nh=nh, ghz=4, ggz=8)
    h1 = _matmul(mix_in, w_out[i].astype(BF16), residual=h, name="out_proj")

    xn2 = _rmsnorm(h1, ffn_norm_w[i])
    qry = _matmul(xn2, peer_w_query[i].astype(BF16), name="peer_query")
    s1, e1, s2, e2, thr = _peer_route(qry, peer_sub_keys[i])
    peer_out = _peer_dense(xn2, peer_u[i].astype(BF16), peer_v[i].astype(BF16), s1, e1, s2, e2, thr)
    h2, xn3 = _add_rmsnorm(h1, peer_out, ple_norm_w[i])

    out = _ple_final(xn3, ple_w_gate[i].astype(BF16), p[i].reshape(t, -1).astype(BF16),
                     ple_w_proj[i].astype(BF16), h2, ple_post_norm_w[i], final_norm_w)
    return out.reshape(bsz, seq, d)
```

```python
import functools

import jax
import jax.numpy as jnp
from jax import lax
from jax.experimental import pallas as pl
from jax.experimental.pallas import tpu as pltpu

F32 = jnp.float32
BF16 = jnp.bfloat16
EPS = 1e-6
HEAD_DIM = 128
CHUNK = 64
SUB = 8
HGRN_GROUP = 8
PEER_TOPK = 16
PEER_NKEYS = 128
LANES = 128
VMEM_LIMIT_BYTES = 56 * 1024 * 1024
HIGHEST = lax.Precision.HIGHEST
NT_DIMS = (((1,), (1,)), ((), ()))
TN_DIMS = (((0,), (0,)), ((), ()))
INV_SQRT2 = 0.7071067811865476


def _params(n_axes, **kw):
    return pltpu.CompilerParams(dimension_semantics=("arbitrary",) * n_axes,
                                vmem_limit_bytes=VMEM_LIMIT_BYTES, **kw)


def _sigmoid(x):
    return 1.0 / (1.0 + jnp.exp(-x))


def _pick(n, prefs):
    for t in prefs:
        if n % t == 0:
            return t
    return n


def _rms_body(x_ref, w_ref, o_ref):
    x = x_ref[...]
    ms = jnp.mean(x * x, axis=-1, keepdims=True)
    o_ref[...] = (x * lax.rsqrt(ms + EPS) * w_ref[...]).astype(o_ref.dtype)


def _rmsnorm(x, w):
    t, d = x.shape
    tm = _pick(t, (512, 256, 128, 64, 8))
    return pl.pallas_call(
        _rms_body,
        grid=(t // tm,),
        in_specs=[pl.BlockSpec((tm, d), lambda i: (i, 0)),
                  pl.BlockSpec((1, d), lambda i: (0, 0))],
        out_specs=pl.BlockSpec((tm, d), lambda i: (i, 0)),
        out_shape=jax.ShapeDtypeStruct((t, d), BF16),
        compiler_params=_params(1),
        name="rmsnorm",
    )(x, w.reshape(1, d))


def _mm_body(a_ref, b_ref, o_ref):
    o_ref[...] = jnp.dot(a_ref[...], b_ref[...], preferred_element_type=F32)


def _mm_res_body(a_ref, b_ref, r_ref, o_ref):
    o_ref[...] = r_ref[...] + jnp.dot(a_ref[...], b_ref[...], preferred_element_type=F32)


def _matmul(a, b, residual=None, name="matmul"):
    m, k = a.shape
    n = b.shape[1]
    tm = _pick(m, (1024, 512, 256, 128, 64, 8))
    tn = _pick(n, (512, 256, 128))
    in_specs = [pl.BlockSpec((tm, k), lambda i, j: (i, 0)),
                pl.BlockSpec((k, tn), lambda i, j: (0, j))]
    args = [a, b]
    body = _mm_body
    if residual is not None:
        in_specs.append(pl.BlockSpec((tm, tn), lambda i, j: (i, j)))
        args.append(residual)
        body = _mm_res_body
    return pl.pallas_call(
        body,
        grid=(m // tm, n // tn),
        in_specs=in_specs,
        out_specs=pl.BlockSpec((tm, tn), lambda i, j: (i, j)),
        out_shape=jax.ShapeDtypeStruct((m, n), F32),
        compiler_params=_params(2),
        name=name,
    )(*args)


def _tri(reverse, strict=False):
    r = lax.broadcasted_iota(jnp.int32, (CHUNK, CHUNK), 0)
    c = lax.broadcasted_iota(jnp.int32, (CHUNK, CHUNK), 1)
    if reverse:
        return (c > r) if strict else (c >= r)
    return (c < r) if strict else (c <= r)


def _hgrn_body(hq_ref, hf_ref, hi_ref, lbp_ref, o_ref, st_ref, q_sc, k_sc, b_sc, *, nh, reverse):
    @pl.when(pl.program_id(0) == 0)
    def _():
        st_ref[...] = jnp.zeros_like(st_ref)

    nl = lbp_ref.shape[0]
    mx = lbp_ref[0]
    for l in range(1, nl):
        mx = jnp.maximum(mx, lbp_ref[l])
    den = jnp.exp(lbp_ref[0] - mx)
    e0 = den
    for l in range(1, nl):
        den = den + jnp.exp(lbp_ref[l] - mx)
    lb = e0 / den

    zq = hq_ref[...]
    s = _sigmoid(hf_ref[...])
    q_sc[...] = zq * _sigmoid(zq) * (HEAD_DIM ** -0.5)
    k_sc[...] = (1.0 - lb) * (1.0 - s)
    logf = jnp.log2(lb + (1.0 - lb) * s)
    b_sc[...] = jnp.dot(_tri(reverse).astype(F32), logf, precision=HIGHEST, preferred_element_type=F32)

    col = lax.broadcasted_iota(jnp.int32, (SUB, CHUNK), 1)
    row8 = lax.broadcasted_iota(jnp.int32, (SUB, 1), 0)
    nsub = CHUNK // SUB
    last = 0 if reverse else CHUNK - 1
    order = [i for i in range(nsub) if i != (nsub - 1 if reverse else 0)]

    for g0 in range(0, nh, HGRN_GROUP):
        heads = list(range(g0, min(g0 + HGRN_GROUP, nh)))
        sls = {h: slice(h * HEAD_DIM, (h + 1) * HEAD_DIM) for h in heads}
        stb = {h: st_ref[h].astype(BF16) for h in heads}
        o = {h: lax.dot_general((q_sc[:, sls[h]] * jnp.exp2(b_sc[:, sls[h]])).astype(BF16), stb[h], NT_DIMS,
                                preferred_element_type=F32) for h in heads}
        cross = {h: {} for h in heads}
        for i in order:
            rb = SUB * (i + 1) if reverse else SUB * i - 1
            rs = slice(SUB * i, SUB * (i + 1))
            keep = (col >= SUB * (i + 1)) if reverse else (col < SUB * i)
            for h in heads:
                ref = b_sc[rb:rb + 1, sls[h]]
                qt = (q_sc[rs, sls[h]] * jnp.exp2(b_sc[rs, sls[h]] - ref)).astype(BF16)
                kh = (k_sc[:, sls[h]] * jnp.exp2(jnp.minimum(ref - b_sc[:, sls[h]], 0.0))).astype(BF16)
                a = lax.dot_general(qt, kh, NT_DIMS, preferred_element_type=F32)
                cross[h][i] = jnp.where(keep, a, 0.0)
        for h in heads:
            a_cross = jnp.concatenate([cross[h].get(i, jnp.zeros((SUB, CHUNK), F32)) for i in range(nsub)],
                                      axis=0)
            o[h] = o[h] + jnp.dot(a_cross.astype(BF16), hi_ref[:, sls[h]].astype(BF16),
                                  preferred_element_type=F32)
        for h in heads:
            diag = []
            for j in range(nsub):
                rs = slice(SUB * j, SUB * (j + 1))
                qj = q_sc[rs, sls[h]]
                bj = b_sc[rs, sls[h]]
                oj = jnp.zeros((SUB, HEAD_DIM), F32)
                for s_loc in range(SUB):
                    r = SUB * j + s_loc
                    pr = qj * k_sc[r:r + 1, sls[h]] * jnp.exp2(bj - b_sc[r:r + 1, sls[h]])
                    a = jnp.sum(pr, axis=1, keepdims=True)
                    keep = (row8 <= s_loc) if reverse else (row8 >= s_loc)
                    oj = oj + jnp.where(keep, a, 0.0) * hi_ref[r:r + 1, sls[h]]
                diag.append(oj)
            o_ref[:, sls[h]] = o[h] + jnp.concatenate(diag, axis=0)
        for h in heads:
            b_last = b_sc[last:last + 1, sls[h]]
            kend = (k_sc[:, sls[h]] * jnp.exp2(b_last - b_sc[:, sls[h]])).astype(BF16)
            st_ref[h] = st_ref[h] * jnp.exp2(b_last) + lax.dot_general(
                hi_ref[:, sls[h]].astype(BF16), kend, TN_DIMS, preferred_element_type=F32)


def _hgrn_scan(proj, lbp, *, nh, gq, gf, gi, reverse):
    t = proj.shape[0]
    w = nh * HEAD_DIM
    nc = t // CHUNK
    cidx = (lambda i: nc - 1 - i) if reverse else (lambda i: i)
    spec = lambda g: pl.BlockSpec((CHUNK, w), lambda i: (cidx(i), g))
    return pl.pallas_call(
        functools.partial(_hgrn_body, nh=nh, reverse=reverse),
        grid=(nc,),
        in_specs=[spec(gq), spec(gf), spec(gi),
                  pl.BlockSpec(lbp.shape, lambda i: (0, 0, 0))],
        out_specs=pl.BlockSpec((CHUNK, w), lambda i: (cidx(i), 0)),
        out_shape=jax.ShapeDtypeStruct((t, w), F32),
        scratch_shapes=[pltpu.VMEM((nh, HEAD_DIM, HEAD_DIM), F32)] + [pltpu.VMEM((CHUNK, w), F32)] * 3,
        compiler_params=_params(1),
        name="hgrn2_bw" if reverse else "hgrn2_fw",
    )(proj, proj, proj, lbp)


HALO = 8


def _conv_body(x_ref, xp_ref, xn_ref, w_ref, o_ref, xe_ref, *, nh, tt, kw):
    i = pl.program_id(0)
    g = pl.program_id(1)
    n = pl.num_programs(0)
    xe_ref[0:HALO] = jnp.where(i > 0, xp_ref[...], 0.0)
    xe_ref[HALO:HALO + tt] = x_ref[...]
    xe_ref[HALO + tt:HALO + tt + HALO] = jnp.where(i < n - 1, xn_ref[...], 0.0)
    pad = (kw - 1) // 2
    acc = None
    for j in range(kw):
        term = w_ref[j:j + 1] * xe_ref[HALO - pad + j:HALO - pad + j + tt]
        acc = term if acc is None else acc + term
    y = acc * _sigmoid(acc)
    for h in range(nh):
        sl = slice(h * HEAD_DIM, (h + 1) * HEAD_DIM)
        yh = y[:, sl]
        inv = lax.rsqrt(jnp.sum(yh * yh, axis=1, keepdims=True) + EPS)
        scale = jnp.where(g == 0, inv * (HEAD_DIM ** -0.5), jnp.where(g == 1, inv, 1.0))
        o_ref[0, :, sl] = yh * scale


def _conv_qkv(proj, conv_w, *, nh, g0):
    t = proj.shape[0]
    w = nh * HEAD_DIM
    kw = conv_w.shape[0]
    tt = _pick(t, (256, 128, 64))
    nb = tt // HALO
    last = t // HALO - 1
    return pl.pallas_call(
        functools.partial(_conv_body, nh=nh, tt=tt, kw=kw),
        grid=(t // tt, 3),
        in_specs=[pl.BlockSpec((tt, w), lambda i, g: (i, g0 + g)),
                  pl.BlockSpec((HALO, w), lambda i, g: (jnp.maximum(i * nb - 1, 0), g0 + g)),
                  pl.BlockSpec((HALO, w), lambda i, g: (jnp.minimum((i + 1) * nb, last), g0 + g)),
                  pl.BlockSpec((kw, w), lambda i, g: (0, g))],
        out_specs=pl.BlockSpec((1, tt, w), lambda i, g: (g, i, 0)),
        out_shape=jax.ShapeDtypeStruct((3, t, w), F32),
        scratch_shapes=[pltpu.VMEM((tt + 2 * HALO, w), F32)],
        compiler_params=_params(2),
        name="gdn_conv",
    )(proj, proj, proj, conv_w)


def _gdn_body(q_ref, k_ref, v_ref, gab_ref, arow_ref, dtrow_ref, o_ref, st_ref, *, nh, reverse):
    @pl.when(pl.program_id(0) == 0)
    def _():
        st_ref[...] = jnp.zeros_like(st_ref)

    gab = gab_ref[...]
    xs = gab + dtrow_ref[...]
    softplus = jnp.maximum(xs, 0.0) + jnp.log(1.0 + jnp.exp(-jnp.abs(xs)))
    dec = -jnp.exp(arow_ref[...]) * softplus
    beta_all = _sigmoid(gab)
    cum = _tri(reverse).astype(F32)
    gam_all = jnp.dot(cum, dec, precision=HIGHEST, preferred_element_type=F32)
    gam_t = gam_all.T
    incl = _tri(reverse)
    strict = _tri(reverse, strict=True)
    r = lax.broadcasted_iota(jnp.int32, (CHUNK, CHUNK), 0)
    c = lax.broadcasted_iota(jnp.int32, (CHUNK, CHUNK), 1)
    eye = (r == c).astype(F32)
    last = 0 if reverse else CHUNK - 1
    d = 1 if reverse else 0

    heads = range(nh)
    sls = [slice(h * HEAD_DIM, (h + 1) * HEAD_DIM) for h in heads]
    gam_c = [gam_all[:, d * nh + h:d * nh + h + 1] for h in heads]
    beta_c = [beta_all[:, 2 * nh + d * nh + h:2 * nh + d * nh + h + 1] for h in heads]
    rel = [jnp.where(incl, jnp.exp(gam_c[h] - gam_t[d * nh + h:d * nh + h + 1, :]), 0.0) for h in heads]
    kb = [k_ref[0, :, sls[h]].astype(BF16) for h in heads]
    kk = [lax.dot_general(kb[h], kb[h], NT_DIMS, preferred_element_type=F32) for h in heads]
    qk = [lax.dot_general(q_ref[0, :, sls[h]].astype(BF16), kb[h], NT_DIMS, preferred_element_type=F32)
          for h in heads]
    qk = [jnp.where(incl, qk[h] * rel[h], 0.0).astype(BF16) for h in heads]
    m = [jnp.where(strict, -(beta_c[h] * kk[h] * rel[h]), 0.0) for h in heads]
    tinv = [eye + m[h] for h in heads]
    pw = [m[h].astype(BF16) for h in heads]
    sq = 1
    while 2 * sq < CHUNK:
        pw = [jnp.dot(pw[h], pw[h], preferred_element_type=F32).astype(BF16) for h in heads]
        tinv = [tinv[h] + jnp.dot(tinv[h].astype(BF16), pw[h], preferred_element_type=F32) for h in heads]
        sq *= 2
    sol = []
    for h in heads:
        k = k_ref[0, :, sls[h]]
        rhs = jnp.concatenate([v_ref[0, :, sls[h]] * beta_c[h], k * (beta_c[h] * jnp.exp(gam_c[h]))], axis=1)
        sol.append(jnp.dot(tinv[h].astype(BF16), rhs.astype(BF16), preferred_element_type=F32))
    stb = [st_ref[h].astype(BF16) for h in heads]
    vnb = [(sol[h][:, :HEAD_DIM]
            - lax.dot_general(sol[h][:, HEAD_DIM:].astype(BF16), stb[h], NT_DIMS, preferred_element_type=F32)
            ).astype(BF16) for h in heads]
    for h in heads:
        qdec = (q_ref[0, :, sls[h]] * jnp.exp(gam_c[h])).astype(BF16)
        o_ref[:, sls[h]] = (lax.dot_general(qdec, stb[h], NT_DIMS, preferred_element_type=F32)
                            + jnp.dot(qk[h], vnb[h], preferred_element_type=F32))
    for h in heads:
        gam_last = gam_c[h][last:last + 1]
        kdec = (k_ref[0, :, sls[h]] * jnp.exp(gam_last - gam_c[h])).astype(BF16)
        st_ref[h] = st_ref[h] * jnp.exp(gam_last) + lax.dot_general(vnb[h], kdec, TN_DIMS,
                                                                    preferred_element_type=F32)


def _gdn_scan(qkv, gab, arow, dtrow, *, nh, reverse):
    t = qkv.shape[1]
    w = nh * HEAD_DIM
    nc = t // CHUNK
    cidx = (lambda i: nc - 1 - i) if reverse else (lambda i: i)
    spec = lambda g: pl.BlockSpec((1, CHUNK, w), lambda i: (g, cidx(i), 0))
    return pl.pallas_call(
        functools.partial(_gdn_body, nh=nh, reverse=reverse),
        grid=(nc,),
        in_specs=[spec(0), spec(1), spec(2),
                  pl.BlockSpec((CHUNK, LANES), lambda i: (cidx(i), 0)),
                  pl.BlockSpec((1, LANES), lambda i: (0, 0)),
                  pl.BlockSpec((1, LANES), lambda i: (0, 0))],
        out_specs=pl.BlockSpec((CHUNK, w), lambda i: (cidx(i), 0)),
        out_shape=jax.ShapeDtypeStruct((t, w), F32),
        scratch_shapes=[pltpu.VMEM((nh, HEAD_DIM, HEAD_DIM), F32)],
        compiler_params=_params(1),
        name="gdn_bw" if reverse else "gdn_fw",
    )(qkv, qkv, qkv, gab, arow, dtrow)


def _gnorm_body(af_ref, ab_ref, bf_ref, bb_ref, hz_ref, gz_ref, wa_ref, wb_ref, o_ref, *, nh):
    w = nh * HEAD_DIM
    for grp, (f_ref, b_ref, z_ref, w_ref) in enumerate(((af_ref, ab_ref, hz_ref, wa_ref),
                                                        (bf_ref, bb_ref, gz_ref, wb_ref))):
        for h in range(nh):
            sl = slice(h * HEAD_DIM, (h + 1) * HEAD_DIM)
            o = f_ref[:, sl] + b_ref[:, sl]
            y = o * lax.rsqrt(jnp.mean(o * o, axis=1, keepdims=True) + EPS) * w_ref[...]
            z = z_ref[:, sl]
            osl = slice(grp * w + h * HEAD_DIM, grp * w + (h + 1) * HEAD_DIM)
            o_ref[:, osl] = (y * (z * _sigmoid(z))).astype(o_ref.dtype)


def _gated_norm(oa_f, oa_b, ob_f, ob_b, proj, wa, wb, *, nh, ghz, ggz):
    t = oa_f.shape[0]
    w = nh * HEAD_DIM
    tm = _pick(t, (256, 128, 64))
    spec = pl.BlockSpec((tm, w), lambda i: (i, 0))
    return pl.pallas_call(
        functools.partial(_gnorm_body, nh=nh),
        grid=(t // tm,),
        in_specs=[spec, spec, spec, spec,
                  pl.BlockSpec((tm, w), lambda i: (i, ghz)),
                  pl.BlockSpec((tm, w), lambda i: (i, ggz)),
                  pl.BlockSpec((1, HEAD_DIM), lambda i: (0, 0)),
                  pl.BlockSpec((1, HEAD_DIM), lambda i: (0, 0))],
        out_specs=pl.BlockSpec((tm, 2 * w), lambda i: (i, 0)),
        out_shape=jax.ShapeDtypeStruct((t, 2 * w), BF16),
        compiler_params=_params(1),
        name="gated_head_norm",
    )(oa_f, oa_b, ob_f, ob_b, proj, proj, wa.reshape(1, HEAD_DIM), wb.reshape(1, HEAD_DIM))


def _cand_rows():
    out = []
    for a in range(PEER_TOPK):
        out.append((a, min(PEER_TOPK, PEER_TOPK // (a + 1))))
    return out


N_CAND = sum(nb for _, nb in _cand_rows())
N_CAND_PAD = -(-N_CAND // 8) * 8


def _route_body(qry_ref, sk_ref, s1_ref, e1_ref, s2_ref, e2_ref, thr_ref, top_ref, cand_ref, *, nheads):
    ninf = -jnp.inf
    for h in range(nheads):
        scores = []
        for p in range(2):
            j = 2 * h + p
            qhp = qry_ref[:, j * PEER_NKEYS:(j + 1) * PEER_NKEYS].astype(BF16)
            keys = sk_ref[h, p].astype(BF16)
            s = lax.dot_general(keys, qhp, NT_DIMS, preferred_element_type=F32)
            scores.append(s)
            wv = s
            for rnk in range(PEER_TOPK):
                mx = jnp.max(wv, axis=0, keepdims=True)
                top_ref[p * PEER_TOPK + rnk:p * PEER_TOPK + rnk + 1, :] = mx
                wv = jnp.where(wv == mx, ninf, wv)
        t1 = top_ref[0:PEER_TOPK]
        t2 = top_ref[PEER_TOPK:2 * PEER_TOPK]
        cand_ref[...] = jnp.full(cand_ref.shape, ninf, F32)
        off = 0
        for a, nb in _cand_rows():
            cand_ref[off:off + nb] = t1[a:a + 1] + t2[0:nb]
            off += nb
        cnd = cand_ref[...]
        cmax = t1[0:1] + t2[0:1]
        z = jnp.zeros_like(cmax)
        mx = cmax
        for rnk in range(PEER_TOPK):
            mx = jnp.max(cnd, axis=0, keepdims=True)
            z = z + jnp.exp(mx - cmax)
            cnd = jnp.where(cnd == mx, ninf, cnd)
        s1, s2 = scores
        s1_ref[h] = s1
        e1_ref[h] = jnp.exp(s1 - t1[0:1]) / z
        s2_ref[h] = s2
        e2_ref[h] = jnp.exp(s2 - t2[0:1])
        thr_ref[h] = mx


def _peer_route(qry, sub_keys):
    t = qry.shape[0]
    nheads = sub_keys.shape[0]
    tt = _pick(t, (512, 256, 128))
    big = pl.BlockSpec((nheads, PEER_NKEYS, tt), lambda i: (0, 0, i))
    big_shape = jax.ShapeDtypeStruct((nheads, PEER_NKEYS, t), F32)
    return pl.pallas_call(
        functools.partial(_route_body, nheads=nheads),
        grid=(t // tt,),
        in_specs=[pl.BlockSpec((tt, qry.shape[1]), lambda i: (i, 0)),
                  pl.BlockSpec(sub_keys.shape, lambda i: (0, 0, 0, 0))],
        out_specs=[big, big, big, big, pl.BlockSpec((nheads, 1, tt), lambda i: (0, 0, i))],
        out_shape=[big_shape, big_shape, big_shape, big_shape,
                   jax.ShapeDtypeStruct((nheads, 1, t), F32)],
        scratch_shapes=[pltpu.VMEM((2 * PEER_TOPK, tt), F32), pltpu.VMEM((N_CAND_PAD, tt), F32)],
        compiler_params=_params(1),
        name="peer_route",
    )(qry, sub_keys)


def _peer_body(xn_ref, u_ref, v_ref, s1_ref, e1_ref, s2_ref, e2_ref, thr_ref, o_ref, g_ref, *, nheads, te):
    @pl.when(pl.program_id(1) == 0)
    def _():
        o_ref[...] = jnp.zeros_like(o_ref)

    nrow = te // PEER_NKEYS
    hid = lax.dot_general(u_ref[...], xn_ref[...], NT_DIMS, preferred_element_type=F32)
    for ii in range(nrow):
        acc = None
        for h in range(nheads):
            s1row = s1_ref[0, h * nrow + ii:h * nrow + ii + 1, :]
            e1row = e1_ref[0, h * nrow + ii:h * nrow + ii + 1, :]
            hit = (s2_ref[h] + s1row) >= thr_ref[h]
            term = jnp.where(hit, e2_ref[h] * e1row, 0.0)
            acc = term if acc is None else acc + term
        g_ref[ii * PEER_NKEYS:(ii + 1) * PEER_NKEYS, :] = acc
    act = (0.5 * hid * (1.0 + lax.erf(hid * INV_SQRT2)) * g_ref[...]).astype(BF16)
    o_ref[...] += lax.dot_general(act, v_ref[...], TN_DIMS, preferred_element_type=F32)


def _peer_dense(xn, u, v, s1, e1, s2, e2, thr):
    t, d = xn.shape
    ne = u.shape[0]
    nheads = s1.shape[0]
    tt = _pick(t, (512, 256, 128))
    te = _pick(ne, (512, 256, 128))
    nrow = te // PEER_NKEYS
    big = pl.BlockSpec((nheads, PEER_NKEYS, tt), lambda i, e: (0, 0, i))
    rows = pl.BlockSpec((1, nheads * nrow, tt), lambda i, e: (e, 0, i))

    def by_block(a):
        return a.reshape(nheads, PEER_NKEYS // nrow, nrow, t).transpose(1, 0, 2, 3).reshape(
            PEER_NKEYS // nrow, nheads * nrow, t)

    s1, e1 = by_block(s1), by_block(e1)
    return pl.pallas_call(
        functools.partial(_peer_body, nheads=nheads, te=te),
        grid=(t // tt, ne // te),
        in_specs=[pl.BlockSpec((tt, d), lambda i, e: (i, 0)),
                  pl.BlockSpec((te, d), lambda i, e: (e, 0)),
                  pl.BlockSpec((te, d), lambda i, e: (e, 0)),
                  rows, rows, big, big,
                  pl.BlockSpec((nheads, 1, tt), lambda i, e: (0, 0, i))],
        out_specs=pl.BlockSpec((tt, d), lambda i, e: (i, 0)),
        out_shape=jax.ShapeDtypeStruct((t, d), F32),
        scratch_shapes=[pltpu.VMEM((te, tt), F32)],
        compiler_params=_params(2),
        name="peer_dense",
    )(xn, u, v, s1, e1, s2, e2, thr)


def _add_rms_body(a_ref, b_ref, w_ref, s_ref, n_ref):
    x = a_ref[...] + b_ref[...]
    s_ref[...] = x
    ms = jnp.mean(x * x, axis=-1, keepdims=True)
    n_ref[...] = (x * lax.rsqrt(ms + EPS) * w_ref[...]).astype(n_ref.dtype)


def _add_rmsnorm(a, b, w):
    t, d = a.shape
    tm = _pick(t, (256, 128, 64, 8))
    spec = pl.BlockSpec((tm, d), lambda i: (i, 0))
    return pl.pallas_call(
        _add_rms_body,
        grid=(t // tm,),
        in_specs=[spec, spec, pl.BlockSpec((1, d), lambda i: (0, 0))],
        out_specs=[spec, spec],
        out_shape=[jax.ShapeDtypeStruct((t, d), F32), jax.ShapeDtypeStruct((t, d), BF16)],
        compiler_params=_params(1),
        name="add_rmsnorm",
    )(a, b, w.reshape(1, d))


def _ple_body(xn_ref, wg_ref, p_ref, wp_ref, h_ref, pw_ref, fw_ref, o_ref, emb_ref, *, tn):
    j = pl.program_id(1)

    @pl.when(j == 0)
    def _():
        raw = jnp.dot(p_ref[...], wp_ref[...], preferred_element_type=F32)
        ms = jnp.mean(raw * raw, axis=-1, keepdims=True)
        emb_ref[...] = raw * lax.rsqrt(ms + EPS) * pw_ref[...]

    cs = pl.ds(pl.multiple_of(j * tn, tn), tn)
    gate = _sigmoid(jnp.dot(xn_ref[...], wg_ref[...], preferred_element_type=F32))
    o_ref[:, cs] = h_ref[...] + gate * emb_ref[:, cs]

    @pl.when(j == pl.num_programs(1) - 1)
    def _():
        x = o_ref[...]
        ms = jnp.mean(x * x, axis=-1, keepdims=True)
        o_ref[...] = x * lax.rsqrt(ms + EPS) * fw_ref[...]


def _ple_final(xn, w_gate, p, w_proj, h, post_w, final_w):
    t, d = h.shape
    pd = p.shape[1]
    tm = _pick(t, (512, 256, 128, 64, 8))
    tn = _pick(d, (512, 256, 128))
    return pl.pallas_call(
        functools.partial(_ple_body, tn=tn),
        grid=(t // tm, d // tn),
        in_specs=[pl.BlockSpec((tm, d), lambda i, j: (i, 0)),
                  pl.BlockSpec((d, tn), lambda i, j: (0, j)),
                  pl.BlockSpec((tm, pd), lambda i, j: (i, 0)),
                  pl.BlockSpec((pd, d), lambda i, j: (0, 0)),
                  pl.BlockSpec((tm, tn), lambda i, j: (i, j)),
                  pl.BlockSpec((1, d), lambda i, j: (0, 0)),
                  pl.BlockSpec((1, d), lambda i, j: (0, 0))],
        out_specs=pl.BlockSpec((tm, d), lambda i, j: (i, 0)),
        out_shape=jax.ShapeDtypeStruct((t, d), F32),
        scratch_shapes=[pltpu.VMEM((tm, d), F32)],
        compiler_params=_params(2),
        name="ple_final",
    )(xn, w_gate, p, w_proj, h, post_w.reshape(1, d), final_w.reshape(1, d))


def kernel(x, p, attn_norm_w, w_in, hg_lower_bound, gd_conv_w, gd_A_log, gd_dt_bias, hg_out_norm_w, gd_out_norm_w, w_out, ffn_norm_w, peer_w_query, peer_sub_keys, peer_u, peer_v, ple_norm_w, ple_w_gate, ple_w_proj, ple_post_norm_w, final_norm_w):
    bsz, seq, d = x.shape
    depth = w_in.shape[0]
    assert bsz == 1 and depth == 1
    w = d // 2
    nh = w // HEAD_DIM
    assert 4 * nh <= LANES
    t = bsz * seq
    h = x.reshape(t, d)
    i = 0

    wi = w_in[i]
    n_main = 8 * w
    w_main = jnp.concatenate([wi[:, :n_main], wi[:, n_main + 4 * nh:]], axis=1).astype(BF16)
    w_gate_cols = jnp.concatenate([wi[:, n_main:n_main + 4 * nh],
                                   jnp.zeros((d, LANES - 4 * nh), wi.dtype)], axis=1).astype(BF16)
    xn = _rmsnorm(h, attn_norm_w[i])
    proj = _matmul(xn, w_main, name="in_proj")
    gab = _matmul(xn, w_gate_cols, name="in_proj_gates")

    oa = []
    for dr, rev in ((0, False), (1, True)):
        lbp = hg_lower_bound[:, dr, :].reshape(depth + 1, 1, w)
        oa.append(_hgrn_scan(proj, lbp, nh=nh, gq=0, gf=1 + dr, gi=3, reverse=rev))

    qkv = _conv_qkv(proj, gd_conv_w[i], nh=nh, g0=5)
    pad = jnp.zeros((LANES - 2 * nh,), F32)
    arow = jnp.concatenate([gd_A_log[i].reshape(-1), pad]).reshape(1, LANES)
    dtrow = jnp.concatenate([gd_dt_bias[i].reshape(-1), pad]).reshape(1, LANES)
    ob = [_gdn_scan(qkv, gab, arow, dtrow, nh=nh, reverse=rev) for rev in (False, True)]

    mix_in = _gated_norm(oa[0], oa[1], ob[0], ob[1], proj, hg_out_norm_w[i], gd_out_norm_w[i],
                         nh=nh, ghz=4, ggz=8)
    h1 = _matmul(mix_in, w_out[i].astype(BF16), residual=h, name="out_proj")

    xn2 = _rmsnorm(h1, ffn_norm_w[i])
    qry = _matmul(xn2, peer_w_query[i].astype(BF16), name="peer_query")
    s1, e1, s2, e2, thr = _peer_route(qry, peer_sub_keys[i])
    peer_out = _peer_dense(xn2, peer_u[i].astype(BF16), peer_v[i].astype(BF16), s1, e1, s2, e2, thr)
    h2, xn3 = _add_rmsnorm(h1, peer_out, ple_norm_w[i])

    out = _ple_final(xn3, ple_w_gate[i].astype(BF16), p[i].reshape(t, -1).astype(BF16),
                     ple_w_proj[i].astype(BF16), h2, ple_post_norm_w[i], final_norm_w)
    return out.reshape(bsz, seq, d)
```

```python
import functools

import jax
import jax.numpy as jnp
from jax import lax
from jax.experimental import pallas as pl
from jax.experimental.pallas import tpu as pltpu

F32 = jnp.float32
BF16 = jnp.bfloat16
EPS = 1e-6
HEAD_DIM = 128
CHUNK = 64
SUB = 8
HGRN_GROUP = 8
PEER_TOPK = 16
PEER_NKEYS = 128
LANES = 128
VMEM_LIMIT_BYTES = 56 * 1024 * 1024
HIGHEST = lax.Precision.HIGHEST
NT_DIMS = (((1,), (1,)), ((), ()))
TN_DIMS = (((0,), (0,)), ((), ()))
INV_SQRT2 = 0.7071067811865476


def _params(n_axes, **kw):
    return pltpu.CompilerParams(dimension_semantics=("arbitrary",) * n_axes,
                                vmem_limit_bytes=VMEM_LIMIT_BYTES, **kw)


def _sigmoid(x):
    return 1.0 / (1.0 + jnp.exp(-x))


def _pick(n, prefs):
    for t in prefs:
        if n % t == 0:
            return t
    return n


def _rms_body(x_ref, w_ref, o_ref):
    x = x_ref[...]
    ms = jnp.mean(x * x, axis=-1, keepdims=True)
    o_ref[...] = (x * lax.rsqrt(ms + EPS) * w_ref[...]).astype(o_ref.dtype)


def _rmsnorm(x, w):
    t, d = x.shape
    tm = _pick(t, (512, 256, 128, 64, 8))
    return pl.pallas_call(
        _rms_body,
        grid=(t // tm,),
        in_specs=[pl.BlockSpec((tm, d), lambda i: (i, 0)),
                  pl.BlockSpec((1, d), lambda i: (0, 0))],
        out_specs=pl.BlockSpec((tm, d), lambda i: (i, 0)),
        out_shape=jax.ShapeDtypeStruct((t, d), BF16),
        compiler_params=_params(1),
        name="rmsnorm",
    )(x, w.reshape(1, d))


def _mm_body(a_ref, b_ref, o_ref):
    o_ref[...] = jnp.dot(a_ref[...], b_ref[...], preferred_element_type=F32)


def _mm_res_body(a_ref, b_ref, r_ref, o_ref):
    o_ref[...] = r_ref[...] + jnp.dot(a_ref[...], b_ref[...], preferred_element_type=F32)


def _matmul(a, b, residual=None, name="matmul"):
    m, k = a.shape
    n = b.shape[1]
    tm = _pick(m, (1024, 512, 256, 128, 64, 8))
    tn = _pick(n, (512, 256, 128))
    in_specs = [pl.BlockSpec((tm, k), lambda i, j: (i, 0)),
                pl.BlockSpec((k, tn), lambda i, j: (0, j))]
    args = [a, b]
    body = _mm_body
    if residual is not None:
        in_specs.append(pl.BlockSpec((tm, tn), lambda i, j: (i, j)))
        args.append(residual)
        body = _mm_res_body
    return pl.pallas_call(
        body,
        grid=(m // tm, n // tn),
        in_specs=in_specs,
        out_specs=pl.BlockSpec((tm, tn), lambda i, j: (i, j)),
        out_shape=jax.ShapeDtypeStruct((m, n), F32),
        compiler_params=_params(2),
        name=name,
    )(*args)


def _tri(reverse, strict=False):
    r = lax.broadcasted_iota(jnp.int32, (CHUNK, CHUNK), 0)
    c = lax.broadcasted_iota(jnp.int32, (CHUNK, CHUNK), 1)
    if reverse:
        return (c > r) if strict else (c >= r)
    return (c < r) if strict else (c <= r)


def _hgrn_body(hq_ref, hf_ref, hi_ref, lbp_ref, o_ref, st_ref, q_sc, k_sc, b_sc, *, nh, reverse):
    @pl.when(pl.program_id(0) == 0)
    def _():
        st_ref[...] = jnp.zeros_like(st_ref)

    nl = lbp_ref.shape[0]
    mx = lbp_ref[0]
    for l in range(1, nl):
        mx = jnp.maximum(mx, lbp_ref[l])
    den = jnp.exp(lbp_ref[0] - mx)
    e0 = den
    for l in range(1, nl):
        den = den + jnp.exp(lbp_ref[l] - mx)
    lb = e0 / den

    zq = hq_ref[...]
    s = _sigmoid(hf_ref[...])
    q_sc[...] = zq * _sigmoid(zq) * (HEAD_DIM ** -0.5)
    k_sc[...] = (1.0 - lb) * (1.0 - s)
    logf = jnp.log2(lb + (1.0 - lb) * s)
    b_sc[...] = jnp.dot(_tri(reverse).astype(F32), logf, precision=HIGHEST, preferred_element_type=F32)

    col = lax.broadcasted_iota(jnp.int32, (SUB, CHUNK), 1)
    row8 = lax.broadcasted_iota(jnp.int32, (SUB, 1), 0)
    nsub = CHUNK // SUB
    last = 0 if reverse else CHUNK - 1
    order = [i for i in range(nsub) if i != (nsub - 1 if reverse else 0)]

    for g0 in range(0, nh, HGRN_GROUP):
        heads = list(range(g0, min(g0 + HGRN_GROUP, nh)))
        sls = {h: slice(h * HEAD_DIM, (h + 1) * HEAD_DIM) for h in heads}
        stb = {h: st_ref[h].astype(BF16) for h in heads}
        o = {h: lax.dot_general((q_sc[:, sls[h]] * jnp.exp2(b_sc[:, sls[h]])).astype(BF16), stb[h], NT_DIMS,
                                preferred_element_type=F32) for h in heads}
        cross = {h: {} for h in heads}
        for i in order:
            rb = SUB * (i + 1) if reverse else SUB * i - 1
            rs = slice(SUB * i, SUB * (i + 1))
            keep = (col >= SUB * (i + 1)) if reverse else (col < SUB * i)
            for h in heads:
                ref = b_sc[rb:rb + 1, sls[h]]
                qt = (q_sc[rs, sls[h]] * jnp.exp2(b_sc[rs, sls[h]] - ref)).astype(BF16)
                kh = (k_sc[:, sls[h]] * jnp.exp2(jnp.minimum(ref - b_sc[:, sls[h]], 0.0))).astype(BF16)
                a = lax.dot_general(qt, kh, NT_DIMS, preferred_element_type=F32)
                cross[h][i] = jnp.where(keep, a, 0.0)
        for h in heads:
            a_cross = jnp.concatenate([cross[h].get(i, jnp.zeros((SUB, CHUNK), F32)) for i in range(nsub)],
                                      axis=0)
            o[h] = o[h] + jnp.dot(a_cross.astype(BF16), hi_ref[:, sls[h]].astype(BF16),
                                  preferred_element_type=F32)
        for h in heads:
            diag = []
            for j in range(nsub):
                rs = slice(SUB * j, SUB * (j + 1))
                qj = q_sc[rs, sls[h]]
                bj = b_sc[rs, sls[h]]
                oj = jnp.zeros((SUB, HEAD_DIM), F32)
                for s_loc in range(SUB):
                    r = SUB * j + s_loc
                    pr = qj * k_sc[r:r + 1, sls[h]] * jnp.exp2(bj - b_sc[r:r + 1, sls[h]])
                    a = jnp.sum(pr, axis=1, keepdims=True)
                    keep = (row8 <= s_loc) if reverse else (row8 >= s_loc)
                    oj = oj + jnp.where(keep, a, 0.0) * hi_ref[r:r + 1, sls[h]]
                diag.append(oj)
            o_ref[:, sls[h]] = o[h] + jnp.concatenate(diag, axis=0)
        for h in heads:
            b_last = b_sc[last:last + 1, sls[h]]
            kend = (k_sc[:, sls[h]] * jnp.exp2(b_last - b_sc[:, sls[h]])).astype(BF16)
            st_ref[h] = st_ref[h] * jnp.exp2(b_last) + lax.dot_general(
                hi_ref[:, sls[h]].astype(BF16), kend, TN_DIMS, preferred_element_type=F32)


def _hgrn_scan(proj, lbp, *, nh, gq, gf, gi, reverse):
    t = proj.shape[0]
    w = nh * HEAD_DIM
    nc = t // CHUNK
    cidx = (lambda i: nc - 1 - i) if reverse else (lambda i: i)
    spec = lambda g: pl.BlockSpec((CHUNK, w), lambda i: (cidx(i), g))
    return pl.pallas_call(
        functools.partial(_hgrn_body, nh=nh, reverse=reverse),
        grid=(nc,),
        in_specs=[spec(gq), spec(gf), spec(gi),
                  pl.BlockSpec(lbp.shape, lambda i: (0, 0, 0))],
        out_specs=pl.BlockSpec((CHUNK, w), lambda i: (cidx(i), 0)),
        out_shape=jax.ShapeDtypeStruct((t, w), F32),
        scratch_shapes=[pltpu.VMEM((nh, HEAD_DIM, HEAD_DIM), F32)] + [pltpu.VMEM((CHUNK, w), F32)] * 3,
        compiler_params=_params(1),
        name="hgrn2_bw" if reverse else "hgrn2_fw",
    )(proj, proj, proj, lbp)


HALO = 8


def _conv_body(x_ref, xp_ref, xn_ref, w_ref, o_ref, xe_ref, *, nh, tt, kw):
    i = pl.program_id(0)
    g = pl.program_id(1)
    n = pl.num_programs(0)
    xe_ref[0:HALO] = jnp.where(i > 0, xp_ref[...], 0.0)
    xe_ref[HALO:HALO + tt] = x_ref[...]
    xe_ref[HALO + tt:HALO + tt + HALO] = jnp.where(i < n - 1, xn_ref[...], 0.0)
    pad = (kw - 1) // 2
    acc = None
    for j in range(kw):
        term = w_ref[j:j + 1] * xe_ref[HALO - pad + j:HALO - pad + j + tt]
        acc = term if acc is None else acc + term
    y = acc * _sigmoid(acc)
    for h in range(nh):
        sl = slice(h * HEAD_DIM, (h + 1) * HEAD_DIM)
        yh = y[:, sl]
        inv = lax.rsqrt(jnp.sum(yh * yh, axis=1, keepdims=True) + EPS)
        scale = jnp.where(g == 0, inv * (HEAD_DIM ** -0.5), jnp.where(g == 1, inv, 1.0))
        o_ref[0, :, sl] = yh * scale


def _conv_qkv(proj, conv_w, *, nh, g0):
    t = proj.shape[0]
    w = nh * HEAD_DIM
    kw = conv_w.shape[0]
    tt = _pick(t, (256, 128, 64))
    nb = tt // HALO
    last = t // HALO - 1
    return pl.pallas_call(
        functools.partial(_conv_body, nh=nh, tt=tt, kw=kw),
        grid=(t // tt, 3),
        in_specs=[pl.BlockSpec((tt, w), lambda i, g: (i, g0 + g)),
                  pl.BlockSpec((HALO, w), lambda i, g: (jnp.maximum(i * nb - 1, 0), g0 + g)),
                  pl.BlockSpec((HALO, w), lambda i, g: (jnp.minimum((i + 1) * nb, last), g0 + g)),
                  pl.BlockSpec((kw, w), lambda i, g: (0, g))],
        out_specs=pl.BlockSpec((1, tt, w), lambda i, g: (g, i, 0)),
        out_shape=jax.ShapeDtypeStruct((3, t, w), F32),
        scratch_shapes=[pltpu.VMEM((tt + 2 * HALO, w), F32)],
        compiler_params=_params(2),
        name="gdn_conv",
    )(proj, proj, proj, conv_w)


def _gdn_body(q_ref, k_ref, v_ref, gab_ref, arow_ref, dtrow_ref, o_ref, st_ref, *, nh, reverse):
    @pl.when(pl.program_id(0) == 0)
    def _():
        st_ref[...] = jnp.zeros_like(st_ref)

    gab = gab_ref[...]
    xs = gab + dtrow_ref[...]
    softplus = jnp.maximum(xs, 0.0) + jnp.log(1.0 + jnp.exp(-jnp.abs(xs)))
    dec = -jnp.exp(arow_ref[...]) * softplus
    beta_all = _sigmoid(gab)
    cum = _tri(reverse).astype(F32)
    gam_all = jnp.dot(cum, dec, precision=HIGHEST, preferred_element_type=F32)
    gam_t = gam_all.T
    incl = _tri(reverse)
    strict = _tri(reverse, strict=True)
    r = lax.broadcasted_iota(jnp.int32, (CHUNK, CHUNK), 0)
    c = lax.broadcasted_iota(jnp.int32, (CHUNK, CHUNK), 1)
    eye = (r == c).astype(F32)
    last = 0 if reverse else CHUNK - 1
    d = 1 if reverse else 0

    heads = range(nh)
    sls = [slice(h * HEAD_DIM, (h + 1) * HEAD_DIM) for h in heads]
    gam_c = [gam_all[:, d * nh + h:d * nh + h + 1] for h in heads]
    beta_c = [beta_all[:, 2 * nh + d * nh + h:2 * nh + d * nh + h + 1] for h in heads]
    rel = [jnp.where(incl, jnp.exp(gam_c[h] - gam_t[d * nh + h:d * nh + h + 1, :]), 0.0) for h in heads]
    kb = [k_ref[0, :, sls[h]].astype(BF16) for h in heads]
    kk = [lax.dot_general(kb[h], kb[h], NT_DIMS, preferred_element_type=F32) for h in heads]
    qk = [lax.dot_general(q_ref[0, :, sls[h]].astype(BF16), kb[h], NT_DIMS, preferred_element_type=F32)
          for h in heads]
    qk = [jnp.where(incl, qk[h] * rel[h], 0.0).astype(BF16) for h in heads]
    m = [jnp.where(strict, -(beta_c[h] * kk[h] * rel[h]), 0.0) for h in heads]
    tinv = [eye + m[h] for h in heads]
    pw = [m[h].astype(BF16) for h in heads]
    sq = 1
    while 2 * sq < CHUNK:
        pw = [jnp.dot(pw[h], pw[h], preferred_element_type=F32).astype(BF16) for h in heads]
        tinv = [tinv[h] + jnp.dot(tinv[h].astype(BF16), pw[h], preferred_element_type=F32) for h in heads]
        sq *= 2
    sol = []
    for h in heads:
        k = k_ref[0, :, sls[h]]
        rhs = jnp.concatenate([v_ref[0, :, sls[h]] * beta_c[h], k * (beta_c[h] * jnp.exp(gam_c[h]))], axis=1)
        sol.append(jnp.dot(tinv[h].astype(BF16), rhs.astype(BF16), preferred_element_type=F32))
    stb = [st_ref[h].astype(BF16) for h in heads]
    vnb = [(sol[h][:, :HEAD_DIM]
            - lax.dot_general(sol[h][:, HEAD_DIM:].astype(BF16), stb[h], NT_DIMS, preferred_element_type=F32)
            ).astype(BF16) for h in heads]
    for h in heads:
        qdec = (q_ref[0, :, sls[h]] * jnp.exp(gam_c[h])).astype(BF16)
        o_ref[:, sls[h]] = (lax.dot_general(qdec, stb[h], NT_DIMS, preferred_element_type=F32)
                            + jnp.dot(qk[h], vnb[h], preferred_element_type=F32))
    for h in heads:
        gam_last = gam_c[h][last:last + 1]
        kdec = (k_ref[0, :, sls[h]] * jnp.exp(gam_last - gam_c[h])).astype(BF16)
        st_ref[h] = st_ref[h] * jnp.exp(gam_last) + lax.dot_general(vnb[h], kdec, TN_DIMS,
                                                                    preferred_element_type=F32)


def _gdn_scan(qkv, gab, arow, dtrow, *, nh, reverse):
    t = qkv.shape[1]
    w = nh * HEAD_DIM
    nc = t // CHUNK
    cidx = (lambda i: nc - 1 - i) if reverse else (lambda i: i)
    spec = lambda g: pl.BlockSpec((1, CHUNK, w), lambda i: (g, cidx(i), 0))
    return pl.pallas_call(
        functools.partial(_gdn_body, nh=nh, reverse=reverse),
        grid=(nc,),
        in_specs=[spec(0), spec(1), spec(2),
                  pl.BlockSpec((CHUNK, LANES), lambda i: (cidx(i), 0)),
                  pl.BlockSpec((1, LANES), lambda i: (0, 0)),
                  pl.BlockSpec((1, LANES), lambda i: (0, 0))],
        out_specs=pl.BlockSpec((CHUNK, w), lambda i: (cidx(i), 0)),
        out_shape=jax.ShapeDtypeStruct((t, w), F32),
        scratch_shapes=[pltpu.VMEM((nh, HEAD_DIM, HEAD_DIM), F32)],
        compiler_params=_params(1),
        name="gdn_bw" if reverse else "gdn_fw",
    )(qkv, qkv, qkv, gab, arow, dtrow)


def _gnorm_body(af_ref, ab_ref, bf_ref, bb_ref, hz_ref, gz_ref, wa_ref, wb_ref, o_ref, *, nh):
    w = nh * HEAD_DIM
    for grp, (f_ref, b_ref, z_ref, w_ref) in enumerate(((af_ref, ab_ref, hz_ref, wa_ref),
                                                        (bf_ref, bb_ref, gz_ref, wb_ref))):
        for h in range(nh):
            sl = slice(h * HEAD_DIM, (h + 1) * HEAD_DIM)
            o = f_ref[:, sl] + b_ref[:, sl]
            y = o * lax.rsqrt(jnp.mean(o * o, axis=1, keepdims=True) + EPS) * w_ref[...]
            z = z_ref[:, sl]
            osl = slice(grp * w + h * HEAD_DIM, grp * w + (h + 1) * HEAD_DIM)
            o_ref[:, osl] = (y * (z * _sigmoid(z))).astype(o_ref.dtype)


def _gated_norm(oa_f, oa_b, ob_f, ob_b, proj, gz, wa, wb, *, nh, ghz):
    t = oa_f.shape[0]
    w = nh * HEAD_DIM
    tm = _pick(t, (256, 128, 64))
    spec = pl.BlockSpec((tm, w), lambda i: (i, 0))
    return pl.pallas_call(
        functools.partial(_gnorm_body, nh=nh),
        grid=(t // tm,),
        in_specs=[spec, spec, spec, spec,
                  pl.BlockSpec((tm, w), lambda i: (i, ghz)),
                  spec,
                  pl.BlockSpec((1, HEAD_DIM), lambda i: (0, 0)),
                  pl.BlockSpec((1, HEAD_DIM), lambda i: (0, 0))],
        out_specs=pl.BlockSpec((tm, 2 * w), lambda i: (i, 0)),
        out_shape=jax.ShapeDtypeStruct((t, 2 * w), BF16),
        compiler_params=_params(1),
        name="gated_head_norm",
    )(oa_f, oa_b, ob_f, ob_b, proj, gz, wa.reshape(1, HEAD_DIM), wb.reshape(1, HEAD_DIM))


def _cand_rows():
    out = []
    for a in range(PEER_TOPK):
        out.append((a, min(PEER_TOPK, PEER_TOPK // (a + 1))))
    return out


N_CAND = sum(nb for _, nb in _cand_rows())
N_CAND_PAD = -(-N_CAND // 8) * 8


def _route_body(qry_ref, sk_ref, c1_ref, e1_ref, r2_ref, e2_ref, top_ref, cand_ref, *, nheads):
    ninf = -jnp.inf
    for h in range(nheads):
        scores, ranks = [], []
        for p in range(2):
            j = 2 * h + p
            qhp = qry_ref[:, j * PEER_NKEYS:(j + 1) * PEER_NKEYS].astype(BF16)
            keys = sk_ref[h, p].astype(BF16)
            s = lax.dot_general(keys, qhp, NT_DIMS, preferred_element_type=F32)
            scores.append(s)
            wv = s
            rk = jnp.full(s.shape, float(PEER_NKEYS), F32)
            for rnk in range(PEER_TOPK):
                mx = jnp.max(wv, axis=0, keepdims=True)
                top_ref[p * PEER_TOPK + rnk:p * PEER_TOPK + rnk + 1, :] = mx
                sel = wv == mx
                rk = jnp.where(sel, float(rnk), rk)
                wv = jnp.where(sel, ninf, wv)
            ranks.append(rk)
        t1 = top_ref[0:PEER_TOPK]
        t2 = top_ref[PEER_TOPK:2 * PEER_TOPK]
        cand_ref[...] = jnp.full(cand_ref.shape, ninf, F32)
        off = 0
        for a, nb in _cand_rows():
            cand_ref[off:off + nb] = t1[a:a + 1] + t2[0:nb]
            off += nb
        cnd = cand_ref[...]
        cmax = t1[0:1] + t2[0:1]
        z = jnp.zeros_like(cmax)
        mx = cmax
        for rnk in range(PEER_TOPK):
            mx = jnp.max(cnd, axis=0, keepdims=True)
            z = z + jnp.exp(mx - cmax)
            cnd = jnp.where(cnd == mx, ninf, cnd)
        s1, s2 = scores
        c1 = jnp.zeros_like(s1)
        for a in range(PEER_TOPK):
            cnt = jnp.sum(jnp.where((t1[a:a + 1] + t2) >= mx, 1.0, 0.0), axis=0, keepdims=True)
            c1 = jnp.where(ranks[0] == float(a), cnt, c1)
        c1_ref[h] = c1
        e1_ref[h] = jnp.exp(s1 - t1[0:1]) / z
        r2_ref[h] = ranks[1].astype(r2_ref.dtype)
        e2_ref[h] = jnp.exp(s2 - t2[0:1]).astype(e2_ref.dtype)


def _peer_route(qry, sub_keys):
    t = qry.shape[0]
    nheads = sub_keys.shape[0]
    tt = _pick(t, (512, 256, 128))
    big = pl.BlockSpec((nheads, PEER_NKEYS, tt), lambda i: (0, 0, i))
    f32_shape = jax.ShapeDtypeStruct((nheads, PEER_NKEYS, t), F32)
    b16_shape = jax.ShapeDtypeStruct((nheads, PEER_NKEYS, t), BF16)
    return pl.pallas_call(
        functools.partial(_route_body, nheads=nheads),
        grid=(t // tt,),
        in_specs=[pl.BlockSpec((tt, qry.shape[1]), lambda i: (i, 0)),
                  pl.BlockSpec(sub_keys.shape, lambda i: (0, 0, 0, 0))],
        out_specs=[big, big, big, big],
        out_shape=[f32_shape, f32_shape, b16_shape, b16_shape],
        scratch_shapes=[pltpu.VMEM((2 * PEER_TOPK, tt), F32), pltpu.VMEM((N_CAND_PAD, tt), F32)],
        compiler_params=_params(1),
        name="peer_route",
    )(qry, sub_keys)


def _peer_body(xn_ref, u_ref, v_ref, c1_ref, e1_ref, r2_ref, e2_ref, o_ref, g_ref, *, nheads, te):
    @pl.when(pl.program_id(1) == 0)
    def _():
        o_ref[...] = jnp.zeros_like(o_ref)

    nrow = te // PEER_NKEYS
    hid = lax.dot_general(u_ref[...], xn_ref[...], NT_DIMS, preferred_element_type=F32)
    zero = jnp.zeros((), g_ref.dtype)
    for ii in range(nrow):
        acc = None
        for h in range(nheads):
            c1row = c1_ref[0, h * nrow + ii:h * nrow + ii + 1, :].astype(g_ref.dtype)
            e1row = e1_ref[0, h * nrow + ii:h * nrow + ii + 1, :].astype(g_ref.dtype)
            term = jnp.where(r2_ref[h] < c1row, e2_ref[h] * e1row, zero)
            acc = term if acc is None else acc + term
        g_ref[ii * PEER_NKEYS:(ii + 1) * PEER_NKEYS, :] = acc
    act = (0.5 * hid * (1.0 + lax.erf(hid * INV_SQRT2)) * g_ref[...].astype(F32)).astype(BF16)
    o_ref[...] += lax.dot_general(act, v_ref[...], TN_DIMS, preferred_element_type=F32)


def _peer_dense(xn, u, v, c1, e1, r2, e2):
    t, d = xn.shape
    ne = u.shape[0]
    nheads = c1.shape[0]
    tt = _pick(t, (512, 256, 128))
    te = _pick(ne, (512, 256, 128))
    nrow = te // PEER_NKEYS
    big = pl.BlockSpec((nheads, PEER_NKEYS, tt), lambda i, e: (0, 0, i))
    rows = pl.BlockSpec((1, nheads * nrow, tt), lambda i, e: (e, 0, i))

    def by_block(a):
        return a.reshape(nheads, PEER_NKEYS // nrow, nrow, t).transpose(1, 0, 2, 3).reshape(
            PEER_NKEYS // nrow, nheads * nrow, t)

    c1, e1 = by_block(c1), by_block(e1)
    return pl.pallas_call(
        functools.partial(_peer_body, nheads=nheads, te=te),
        grid=(t // tt, ne // te),
        in_specs=[pl.BlockSpec((tt, d), lambda i, e: (i, 0)),
                  pl.BlockSpec((te, d), lambda i, e: (e, 0)),
                  pl.BlockSpec((te, d), lambda i, e: (e, 0)),
                  rows, rows, big, big],
        out_specs=pl.BlockSpec((tt, d), lambda i, e: (i, 0)),
        out_shape=jax.ShapeDtypeStruct((t, d), F32),
        scratch_shapes=[pltpu.VMEM((te, tt), BF16)],
        compiler_params=_params(2),
        name="peer_dense",
    )(xn, u, v, c1, e1, r2, e2)


def _add_rms_body(a_ref, b_ref, w_ref, s_ref, n_ref):
    x = a_ref[...] + b_ref[...]
    s_ref[...] = x
    ms = jnp.mean(x * x, axis=-1, keepdims=True)
    n_ref[...] = (x * lax.rsqrt(ms + EPS) * w_ref[...]).astype(n_ref.dtype)


def _add_rmsnorm(a, b, w):
    t, d = a.shape
    tm = _pick(t, (256, 128, 64, 8))
    spec = pl.BlockSpec((tm, d), lambda i: (i, 0))
    return pl.pallas_call(
        _add_rms_body,
        grid=(t // tm,),
        in_specs=[spec, spec, pl.BlockSpec((1, d), lambda i: (0, 0))],
        out_specs=[spec, spec],
        out_shape=[jax.ShapeDtypeStruct((t, d), F32), jax.ShapeDtypeStruct((t, d), BF16)],
        compiler_params=_params(1),
        name="add_rmsnorm",
    )(a, b, w.reshape(1, d))


def _ple_body(xn_ref, wg_ref, p_ref, wp_ref, h_ref, pw_ref, fw_ref, o_ref, emb_ref, *, tn):
    j = pl.program_id(1)

    @pl.when(j == 0)
    def _():
        raw = jnp.dot(p_ref[...], wp_ref[...], preferred_element_type=F32)
        ms = jnp.mean(raw * raw, axis=-1, keepdims=True)
        emb_ref[...] = raw * lax.rsqrt(ms + EPS) * pw_ref[...]

    cs = pl.ds(pl.multiple_of(j * tn, tn), tn)
    gate = _sigmoid(jnp.dot(xn_ref[...], wg_ref[...], preferred_element_type=F32))
    o_ref[:, cs] = h_ref[...] + gate * emb_ref[:, cs]

    @pl.when(j == pl.num_programs(1) - 1)
    def _():
        x = o_ref[...]
        ms = jnp.mean(x * x, axis=-1, keepdims=True)
        o_ref[...] = x * lax.rsqrt(ms + EPS) * fw_ref[...]


def _ple_final(xn, w_gate, p, w_proj, h, post_w, final_w):
    t, d = h.shape
    pd = p.shape[1]
    tm = _pick(t, (512, 256, 128, 64, 8))
    tn = _pick(d, (512, 256, 128))
    return pl.pallas_call(
        functools.partial(_ple_body, tn=tn),
        grid=(t // tm, d // tn),
        in_specs=[pl.BlockSpec((tm, d), lambda i, j: (i, 0)),
                  pl.BlockSpec((d, tn), lambda i, j: (0, j)),
                  pl.BlockSpec((tm, pd), lambda i, j: (i, 0)),
                  pl.BlockSpec((pd, d), lambda i, j: (0, 0)),
                  pl.BlockSpec((tm, tn), lambda i, j: (i, j)),
                  pl.BlockSpec((1, d), lambda i, j: (0, 0)),
                  pl.BlockSpec((1, d), lambda i, j: (0, 0))],
        out_specs=pl.BlockSpec((tm, d), lambda i, j: (i, 0)),
        out_shape=jax.ShapeDtypeStruct((t, d), F32),
        scratch_shapes=[pltpu.VMEM((tm, d), F32)],
        compiler_params=_params(2),
        name="ple_final",
    )(xn, w_gate, p, w_proj, h, post_w.reshape(1, d), final_w.reshape(1, d))


def kernel(x, p, attn_norm_w, w_in, hg_lower_bound, gd_conv_w, gd_A_log, gd_dt_bias, hg_out_norm_w, gd_out_norm_w, w_out, ffn_norm_w, peer_w_query, peer_sub_keys, peer_u, peer_v, ple_norm_w, ple_w_gate, ple_w_proj, ple_post_norm_w, final_norm_w):
    bsz, seq, d = x.shape
    depth = w_in.shape[0]
    assert bsz == 1 and depth == 1
    w = d // 2
    nh = w // HEAD_DIM
    assert 4 * nh <= LANES
    t = bsz * seq
    h = x.reshape(t, d)
    i = 0

    wi = w_in[i]
    n_main = 8 * w
    w_gate_cols = jnp.concatenate([wi[:, n_main:n_main + 4 * nh],
                                   jnp.zeros((d, LANES - 4 * nh), wi.dtype)], axis=1).astype(BF16)
    xn = _rmsnorm(h, attn_norm_w[i])
    proj = _matmul(xn, wi[:, :n_main].astype(BF16), name="in_proj")
    gab = _matmul(xn, w_gate_cols, name="in_proj_gates")
    gz = _matmul(xn, wi[:, n_main + 4 * nh:].astype(BF16), name="in_proj_gz")

    oa = []
    for dr, rev in ((0, False), (1, True)):
        lbp = hg_lower_bound[:, dr, :].reshape(depth + 1, 1, w)
        oa.append(_hgrn_scan(proj, lbp, nh=nh, gq=0, gf=1 + dr, gi=3, reverse=rev))

    qkv = _conv_qkv(proj, gd_conv_w[i], nh=nh, g0=5)
    pad = jnp.zeros((LANES - 2 * nh,), F32)
    arow = jnp.concatenate([gd_A_log[i].reshape(-1), pad]).reshape(1, LANES)
    dtrow = jnp.concatenate([gd_dt_bias[i].reshape(-1), pad]).reshape(1, LANES)
    ob = [_gdn_scan(qkv, gab, arow, dtrow, nh=nh, reverse=rev) for rev in (False, True)]

    mix_in = _gated_norm(oa[0], oa[1], ob[0], ob[1], proj, gz, hg_out_norm_w[i], gd_out_norm_w[i],
                         nh=nh, ghz=4)
    h1 = _matmul(mix_in, w_out[i].astype(BF16), residual=h, name="out_proj")

    xn2 = _rmsnorm(h1, ffn_norm_w[i])
    qry = _matmul(xn2, peer_w_query[i].astype(BF16), name="peer_query")
    c1, e1, r2, e2 = _peer_route(qry, peer_sub_keys[i])
    peer_out = _peer_dense(xn2, peer_u[i].astype(BF16), peer_v[i].astype(BF16), c1, e1, r2, e2)
    h2, xn3 = _add_rmsnorm(h1, peer_out, ple_norm_w[i])

    out = _ple_final(xn3, ple_w_gate[i].astype(BF16), p[i].reshape(t, -1).astype(BF16),
                     ple_w_proj[i].astype(BF16), h2, ple_post_norm_w[i], final_norm_w)
    return out.reshape(bsz, seq, d)
```

```python
import functools

import jax
import jax.numpy as jnp
from jax import lax
from jax.experimental import pallas as pl
from jax.experimental.pallas import tpu as pltpu

F32 = jnp.float32
BF16 = jnp.bfloat16
EPS = 1e-6
HEAD_DIM = 128
CHUNK = 64
SUB = 8
HGRN_GROUP = 8
PEER_TOPK = 16
PEER_NKEYS = 128
LANES = 128
ROW_TILE = 8
VMEM_LIMIT_BYTES = 56 * 1024 * 1024
HIGHEST = lax.Precision.HIGHEST
NT_DIMS = (((1,), (1,)), ((), ()))
TN_DIMS = (((0,), (0,)), ((), ()))
INV_SQRT2 = 0.7071067811865476


def _params(n_axes, **kw):
    return pltpu.CompilerParams(dimension_semantics=("arbitrary",) * n_axes,
                                vmem_limit_bytes=VMEM_LIMIT_BYTES, **kw)


def _sigmoid(x):
    return 1.0 / (1.0 + jnp.exp(-x))


def _pick(n, prefs):
    for t in prefs:
        if n % t == 0:
            return t
    return n


def _rms_body(x_ref, w_ref, o_ref):
    x = x_ref[...]
    ms = jnp.mean(x * x, axis=-1, keepdims=True)
    o_ref[...] = (x * lax.rsqrt(ms + EPS) * w_ref[...]).astype(o_ref.dtype)


def _rmsnorm(x, w):
    t, d = x.shape
    tm = _pick(t, (512, 256, 128, 64, 8))
    return pl.pallas_call(
        _rms_body,
        grid=(t // tm,),
        in_specs=[pl.BlockSpec((tm, d), lambda i: (i, 0)),
                  pl.BlockSpec((1, d), lambda i: (0, 0))],
        out_specs=pl.BlockSpec((tm, d), lambda i: (i, 0)),
        out_shape=jax.ShapeDtypeStruct((t, d), BF16),
        compiler_params=_params(1),
        name="rmsnorm",
    )(x, w.reshape(1, d))


def _mm_body(a_ref, b_ref, o_ref):
    o_ref[...] = jnp.dot(a_ref[...], b_ref[...], preferred_element_type=F32)


def _mm_res_body(a_ref, b_ref, r_ref, o_ref):
    o_ref[...] = r_ref[...] + jnp.dot(a_ref[...], b_ref[...], preferred_element_type=F32)


def _matmul(a, b, residual=None, name="matmul", col0=0, ncols=None):
    m, k = a.shape
    n = b.shape[1] if ncols is None else ncols
    tm = _pick(m, (1024, 512, 256, 128, 64, 8))
    tn = _pick(n, (512, 256, 128))
    assert col0 % tn == 0
    jb = col0 // tn
    in_specs = [pl.BlockSpec((tm, k), lambda i, j: (i, 0)),
                pl.BlockSpec((k, tn), lambda i, j: (0, jb + j))]
    args = [a, b]
    body = _mm_body
    if residual is not None:
        in_specs.append(pl.BlockSpec((tm, tn), lambda i, j: (i, j)))
        args.append(residual)
        body = _mm_res_body
    return pl.pallas_call(
        body,
        grid=(m // tm, n // tn),
        in_specs=in_specs,
        out_specs=pl.BlockSpec((tm, tn), lambda i, j: (i, j)),
        out_shape=jax.ShapeDtypeStruct((m, n), F32),
        compiler_params=_params(2),
        name=name,
    )(*args)


def _tri(reverse, strict=False):
    r = lax.broadcasted_iota(jnp.int32, (CHUNK, CHUNK), 0)
    c = lax.broadcasted_iota(jnp.int32, (CHUNK, CHUNK), 1)
    if reverse:
        return (c > r) if strict else (c >= r)
    return (c < r) if strict else (c <= r)


def _hgrn_body(hq_ref, hf_ref, hi_ref, lbp_ref, o_ref, st_ref, q_sc, k_sc, b_sc, *, nh, reverse):
    @pl.when(pl.program_id(0) == 0)
    def _():
        st_ref[...] = jnp.zeros_like(st_ref)

    nl = lbp_ref.shape[0]
    mx = lbp_ref[0]
    for l in range(1, nl):
        mx = jnp.maximum(mx, lbp_ref[l])
    den = jnp.exp(lbp_ref[0] - mx)
    e0 = den
    for l in range(1, nl):
        den = den + jnp.exp(lbp_ref[l] - mx)
    lb = e0 / den

    zq = hq_ref[...]
    s = _sigmoid(hf_ref[...])
    q_sc[...] = zq * _sigmoid(zq) * (HEAD_DIM ** -0.5)
    k_sc[...] = (1.0 - lb) * (1.0 - s)
    logf = jnp.log2(lb + (1.0 - lb) * s)
    b_sc[...] = jnp.dot(_tri(reverse).astype(F32), logf, precision=HIGHEST, preferred_element_type=F32)

    col = lax.broadcasted_iota(jnp.int32, (SUB, CHUNK), 1)
    row8 = lax.broadcasted_iota(jnp.int32, (SUB, 1), 0)
    nsub = CHUNK // SUB
    last = 0 if reverse else CHUNK - 1
    order = [i for i in range(nsub) if i != (nsub - 1 if reverse else 0)]

    for g0 in range(0, nh, HGRN_GROUP):
        heads = list(range(g0, min(g0 + HGRN_GROUP, nh)))
        sls = {h: slice(h * HEAD_DIM, (h + 1) * HEAD_DIM) for h in heads}
        stb = {h: st_ref[h].astype(BF16) for h in heads}
        o = {h: lax.dot_general((q_sc[:, sls[h]] * jnp.exp2(b_sc[:, sls[h]])).astype(BF16), stb[h], NT_DIMS,
                                preferred_element_type=F32) for h in heads}
        cross = {h: {} for h in heads}
        for i in order:
            rb = SUB * (i + 1) if reverse else SUB * i - 1
            rs = slice(SUB * i, SUB * (i + 1))
            keep = (col >= SUB * (i + 1)) if reverse else (col < SUB * i)
            for h in heads:
                ref = b_sc[rb:rb + 1, sls[h]]
                qt = (q_sc[rs, sls[h]] * jnp.exp2(b_sc[rs, sls[h]] - ref)).astype(BF16)
                kh = (k_sc[:, sls[h]] * jnp.exp2(jnp.minimum(ref - b_sc[:, sls[h]], 0.0))).astype(BF16)
                a = lax.dot_general(qt, kh, NT_DIMS, preferred_element_type=F32)
                cross[h][i] = jnp.where(keep, a, 0.0)
        for h in heads:
            a_cross = jnp.concatenate([cross[h].get(i, jnp.zeros((SUB, CHUNK), F32)) for i in range(nsub)],
                                      axis=0)
            o[h] = o[h] + jnp.dot(a_cross.astype(BF16), hi_ref[:, sls[h]].astype(BF16),
                                  preferred_element_type=F32)
        for h in heads:
            diag = []
            for j in range(nsub):
                rs = slice(SUB * j, SUB * (j + 1))
                qj = q_sc[rs, sls[h]]
                bj = b_sc[rs, sls[h]]
                oj = jnp.zeros((SUB, HEAD_DIM), F32)
                for s_loc in range(SUB):
                    r = SUB * j + s_loc
                    pr = qj * k_sc[r:r + 1, sls[h]] * jnp.exp2(bj - b_sc[r:r + 1, sls[h]])
                    a = jnp.sum(pr, axis=1, keepdims=True)
                    keep = (row8 <= s_loc) if reverse else (row8 >= s_loc)
                    oj = oj + jnp.where(keep, a, 0.0) * hi_ref[r:r + 1, sls[h]]
                diag.append(oj)
            o_ref[:, sls[h]] = o[h] + jnp.concatenate(diag, axis=0)
        for h in heads:
            b_last = b_sc[last:last + 1, sls[h]]
            kend = (k_sc[:, sls[h]] * jnp.exp2(b_last - b_sc[:, sls[h]])).astype(BF16)
            st_ref[h] = st_ref[h] * jnp.exp2(b_last) + lax.dot_general(
                hi_ref[:, sls[h]].astype(BF16), kend, TN_DIMS, preferred_element_type=F32)


def _hgrn_scan(proj, lbp, *, nh, gq, gf, gi, reverse):
    t = proj.shape[0]
    w = nh * HEAD_DIM
    nc = t // CHUNK
    cidx = (lambda i: nc - 1 - i) if reverse else (lambda i: i)
    spec = lambda g: pl.BlockSpec((CHUNK, w), lambda i: (cidx(i), g))
    return pl.pallas_call(
        functools.partial(_hgrn_body, nh=nh, reverse=reverse),
        grid=(nc,),
        in_specs=[spec(gq), spec(gf), spec(gi),
                  pl.BlockSpec(lbp.shape, lambda i: (0, 0, 0))],
        out_specs=pl.BlockSpec((CHUNK, w), lambda i: (cidx(i), 0)),
        out_shape=jax.ShapeDtypeStruct((t, w), F32),
        scratch_shapes=[pltpu.VMEM((nh, HEAD_DIM, HEAD_DIM), F32)] + [pltpu.VMEM((CHUNK, w), F32)] * 3,
        compiler_params=_params(1),
        name="hgrn2_bw" if reverse else "hgrn2_fw",
    )(proj, proj, proj, lbp)


HALO = 8


def _conv_body(x_ref, xp_ref, xn_ref, w_ref, o_ref, xe_ref, *, nh, tt, kw):
    i = pl.program_id(0)
    g = pl.program_id(1)
    n = pl.num_programs(0)
    xe_ref[0:HALO] = jnp.where(i > 0, xp_ref[...], 0.0)
    xe_ref[HALO:HALO + tt] = x_ref[...]
    xe_ref[HALO + tt:HALO + tt + HALO] = jnp.where(i < n - 1, xn_ref[...], 0.0)
    pad = (kw - 1) // 2
    acc = None
    for j in range(kw):
        term = w_ref[j:j + 1] * xe_ref[HALO - pad + j:HALO - pad + j + tt]
        acc = term if acc is None else acc + term
    y = acc * _sigmoid(acc)
    for h in range(nh):
        sl = slice(h * HEAD_DIM, (h + 1) * HEAD_DIM)
        yh = y[:, sl]
        inv = lax.rsqrt(jnp.sum(yh * yh, axis=1, keepdims=True) + EPS)
        scale = jnp.where(g == 0, inv * (HEAD_DIM ** -0.5), jnp.where(g == 1, inv, 1.0))
        o_ref[0, :, sl] = yh * scale


def _conv_qkv(proj, conv_w, *, nh, g0):
    t = proj.shape[0]
    w = nh * HEAD_DIM
    kw = conv_w.shape[0]
    tt = _pick(t, (256, 128, 64))
    nb = tt // HALO
    last = t // HALO - 1
    return pl.pallas_call(
        functools.partial(_conv_body, nh=nh, tt=tt, kw=kw),
        grid=(t // tt, 3),
        in_specs=[pl.BlockSpec((tt, w), lambda i, g: (i, g0 + g)),
                  pl.BlockSpec((HALO, w), lambda i, g: (jnp.maximum(i * nb - 1, 0), g0 + g)),
                  pl.BlockSpec((HALO, w), lambda i, g: (jnp.minimum((i + 1) * nb, last), g0 + g)),
                  pl.BlockSpec((kw, w), lambda i, g: (0, g))],
        out_specs=pl.BlockSpec((1, tt, w), lambda i, g: (g, i, 0)),
        out_shape=jax.ShapeDtypeStruct((3, t, w), F32),
        scratch_shapes=[pltpu.VMEM((tt + 2 * HALO, w), F32)],
        compiler_params=_params(2),
        name="gdn_conv",
    )(proj, proj, proj, conv_w)


def _gdn_body(q_ref, k_ref, v_ref, gab_ref, arow_ref, dtrow_ref, o_ref, st_ref, *, nh, reverse):
    @pl.when(pl.program_id(0) == 0)
    def _():
        st_ref[...] = jnp.zeros_like(st_ref)

    gab = gab_ref[...]
    xs = gab + dtrow_ref[...]
    softplus = jnp.maximum(xs, 0.0) + jnp.log(1.0 + jnp.exp(-jnp.abs(xs)))
    dec = -jnp.exp(arow_ref[...]) * softplus
    beta_all = _sigmoid(gab)
    cum = _tri(reverse).astype(F32)
    gam_all = jnp.dot(cum, dec, precision=HIGHEST, preferred_element_type=F32)
    gam_t = gam_all.T
    incl = _tri(reverse)
    strict = _tri(reverse, strict=True)
    r = lax.broadcasted_iota(jnp.int32, (CHUNK, CHUNK), 0)
    c = lax.broadcasted_iota(jnp.int32, (CHUNK, CHUNK), 1)
    eye = (r == c).astype(F32)
    last = 0 if reverse else CHUNK - 1
    d = 1 if reverse else 0

    heads = range(nh)
    sls = [slice(h * HEAD_DIM, (h + 1) * HEAD_DIM) for h in heads]
    gam_c = [gam_all[:, d * nh + h:d * nh + h + 1] for h in heads]
    beta_c = [beta_all[:, 2 * nh + d * nh + h:2 * nh + d * nh + h + 1] for h in heads]
    rel = [jnp.where(incl, jnp.exp(gam_c[h] - gam_t[d * nh + h:d * nh + h + 1, :]), 0.0) for h in heads]
    kb = [k_ref[0, :, sls[h]].astype(BF16) for h in heads]
    kk = [lax.dot_general(kb[h], kb[h], NT_DIMS, preferred_element_type=F32) for h in heads]
    qk = [lax.dot_general(q_ref[0, :, sls[h]].astype(BF16), kb[h], NT_DIMS, preferred_element_type=F32)
          for h in heads]
    qk = [jnp.where(incl, qk[h] * rel[h], 0.0).astype(BF16) for h in heads]
    m = [jnp.where(strict, -(beta_c[h] * kk[h] * rel[h]), 0.0) for h in heads]
    tinv = [eye + m[h] for h in heads]
    pw = [m[h].astype(BF16) for h in heads]
    sq = 1
    while 2 * sq < CHUNK:
        pw = [jnp.dot(pw[h], pw[h], preferred_element_type=F32).astype(BF16) for h in heads]
        tinv = [tinv[h] + jnp.dot(tinv[h].astype(BF16), pw[h], preferred_element_type=F32) for h in heads]
        sq *= 2
    sol = []
    for h in heads:
        k = k_ref[0, :, sls[h]]
        rhs = jnp.concatenate([v_ref[0, :, sls[h]] * beta_c[h], k * (beta_c[h] * jnp.exp(gam_c[h]))], axis=1)
        sol.append(jnp.dot(tinv[h].astype(BF16), rhs.astype(BF16), preferred_element_type=F32))
    stb = [st_ref[h].astype(BF16) for h in heads]
    vnb = [(sol[h][:, :HEAD_DIM]
            - lax.dot_general(sol[h][:, HEAD_DIM:].astype(BF16), stb[h], NT_DIMS, preferred_element_type=F32)
            ).astype(BF16) for h in heads]
    for h in heads:
        qdec = (q_ref[0, :, sls[h]] * jnp.exp(gam_c[h])).astype(BF16)
        o_ref[:, sls[h]] = (lax.dot_general(qdec, stb[h], NT_DIMS, preferred_element_type=F32)
                            + jnp.dot(qk[h], vnb[h], preferred_element_type=F32))
    for h in heads:
        gam_last = gam_c[h][last:last + 1]
        kdec = (k_ref[0, :, sls[h]] * jnp.exp(gam_last - gam_c[h])).astype(BF16)
        st_ref[h] = st_ref[h] * jnp.exp(gam_last) + lax.dot_general(vnb[h], kdec, TN_DIMS,
                                                                    preferred_element_type=F32)


def _gdn_scan(qkv, gab, arow, dtrow, *, nh, reverse):
    t = qkv.shape[1]
    w = nh * HEAD_DIM
    nc = t // CHUNK
    cidx = (lambda i: nc - 1 - i) if reverse else (lambda i: i)
    spec = lambda g: pl.BlockSpec((1, CHUNK, w), lambda i: (g, cidx(i), 0))
    return pl.pallas_call(
        functools.partial(_gdn_body, nh=nh, reverse=reverse),
        grid=(nc,),
        in_specs=[spec(0), spec(1), spec(2),
                  pl.BlockSpec((CHUNK, LANES), lambda i: (cidx(i), 0)),
                  pl.BlockSpec((1, LANES), lambda i: (0, 0)),
                  pl.BlockSpec((1, LANES), lambda i: (0, 0))],
        out_specs=pl.BlockSpec((CHUNK, w), lambda i: (cidx(i), 0)),
        out_shape=jax.ShapeDtypeStruct((t, w), F32),
        scratch_shapes=[pltpu.VMEM((nh, HEAD_DIM, HEAD_DIM), F32)],
        compiler_params=_params(1),
        name="gdn_bw" if reverse else "gdn_fw",
    )(qkv, qkv, qkv, gab, arow, dtrow)


def _gnorm_body(af_ref, ab_ref, bf_ref, bb_ref, hz_ref, gz_ref, wa_ref, wb_ref, o_ref, *, nh):
    w = nh * HEAD_DIM
    for grp, (f_ref, b_ref, z_ref, w_ref) in enumerate(((af_ref, ab_ref, hz_ref, wa_ref),
                                                        (bf_ref, bb_ref, gz_ref, wb_ref))):
        for h in range(nh):
            sl = slice(h * HEAD_DIM, (h + 1) * HEAD_DIM)
            o = f_ref[:, sl] + b_ref[:, sl]
            y = o * lax.rsqrt(jnp.mean(o * o, axis=1, keepdims=True) + EPS) * w_ref[...]
            z = z_ref[:, sl]
            osl = slice(grp * w + h * HEAD_DIM, grp * w + (h + 1) * HEAD_DIM)
            o_ref[:, osl] = (y * (z * _sigmoid(z))).astype(o_ref.dtype)


def _gated_norm(oa_f, oa_b, ob_f, ob_b, proj, gz, wa, wb, *, nh, ghz):
    t = oa_f.shape[0]
    w = nh * HEAD_DIM
    tm = _pick(t, (256, 128, 64))
    spec = pl.BlockSpec((tm, w), lambda i: (i, 0))
    return pl.pallas_call(
        functools.partial(_gnorm_body, nh=nh),
        grid=(t // tm,),
        in_specs=[spec, spec, spec, spec,
                  pl.BlockSpec((tm, w), lambda i: (i, ghz)),
                  spec,
                  pl.BlockSpec((1, HEAD_DIM), lambda i: (0, 0)),
                  pl.BlockSpec((1, HEAD_DIM), lambda i: (0, 0))],
        out_specs=pl.BlockSpec((tm, 2 * w), lambda i: (i, 0)),
        out_shape=jax.ShapeDtypeStruct((t, 2 * w), BF16),
        compiler_params=_params(1),
        name="gated_head_norm",
    )(oa_f, oa_b, ob_f, ob_b, proj, gz, wa.reshape(1, HEAD_DIM), wb.reshape(1, HEAD_DIM))


def _cand_rows():
    out = []
    for a in range(PEER_TOPK):
        out.append((a, min(PEER_TOPK, PEER_TOPK // (a + 1))))
    return out


N_CAND = sum(nb for _, nb in _cand_rows())
N_CAND_PAD = -(-N_CAND // 8) * 8


def _route_body(qry_ref, sk_ref, c1_ref, e1_ref, r2_ref, e2_ref, top_ref, cand_ref, *, nheads):
    ninf = -jnp.inf
    for h in range(nheads):
        scores, ranks = [], []
        for p in range(2):
            j = 2 * h + p
            qhp = qry_ref[:, j * PEER_NKEYS:(j + 1) * PEER_NKEYS].astype(BF16)
            keys = sk_ref[h, p].astype(BF16)
            s = lax.dot_general(keys, qhp, NT_DIMS, preferred_element_type=F32)
            scores.append(s)
            wv = s
            rk = jnp.full(s.shape, float(PEER_NKEYS), F32)
            for rnk in range(PEER_TOPK):
                mx = jnp.max(wv, axis=0, keepdims=True)
                top_ref[p * PEER_TOPK + rnk:p * PEER_TOPK + rnk + 1, :] = mx
                sel = wv == mx
                if p == 1:
                    rk = jnp.where(sel, float(rnk), rk)
                wv = jnp.where(sel, ninf, wv)
            ranks.append(rk)
        t1 = top_ref[0:PEER_TOPK]
        t2 = top_ref[PEER_TOPK:2 * PEER_TOPK]
        cand_ref[...] = jnp.full(cand_ref.shape, ninf, F32)
        off = 0
        for a, nb in _cand_rows():
            cand_ref[off:off + nb] = t1[a:a + 1] + t2[0:nb]
            off += nb
        cnd = cand_ref[...]
        cmax = t1[0:1] + t2[0:1]
        z = jnp.zeros_like(cmax)
        mx = cmax
        for rnk in range(PEER_TOPK):
            mx = jnp.max(cnd, axis=0, keepdims=True)
            z = z + jnp.exp(mx - cmax)
            cnd = jnp.where(cnd == mx, ninf, cnd)
        s1, s2 = scores
        c1 = jnp.zeros_like(s1)
        for a in range(PEER_TOPK):
            cnt = jnp.sum(jnp.where((t1[a:a + 1] + t2) >= mx, 1.0, 0.0), axis=0, keepdims=True)
            c1 = jnp.where(s1 == t1[a:a + 1], cnt, c1)
        c1_ref[h] = c1
        e1_ref[h] = jnp.exp(s1 - t1[0:1]) / z
        r2_ref[h] = ranks[1].astype(r2_ref.dtype)
        e2_ref[h] = jnp.exp(s2 - t2[0:1]).astype(e2_ref.dtype)


def _peer_route(qry, sub_keys):
    t = qry.shape[0]
    nheads = sub_keys.shape[0]
    tt = _pick(t, (512, 256, 128))
    big = pl.BlockSpec((nheads, PEER_NKEYS, tt), lambda i: (0, 0, i))
    f32_shape = jax.ShapeDtypeStruct((nheads, PEER_NKEYS, t), F32)
    b16_shape = jax.ShapeDtypeStruct((nheads, PEER_NKEYS, t), BF16)
    return pl.pallas_call(
        functools.partial(_route_body, nheads=nheads),
        grid=(t // tt,),
        in_specs=[pl.BlockSpec((tt, qry.shape[1]), lambda i: (i, 0)),
                  pl.BlockSpec(sub_keys.shape, lambda i: (0, 0, 0, 0))],
        out_specs=[big, big, big, big],
        out_shape=[f32_shape, f32_shape, b16_shape, b16_shape],
        scratch_shapes=[pltpu.VMEM((2 * PEER_TOPK, tt), F32), pltpu.VMEM((N_CAND_PAD, tt), F32)],
        compiler_params=_params(1),
        name="peer_route",
    )(qry, sub_keys)


def _peer_body(xn_ref, u_ref, v_ref, c1_ref, e1_ref, r2_ref, e2_ref, o_ref, g_ref, *, nheads, te):
    @pl.when(pl.program_id(1) == 0)
    def _():
        o_ref[...] = jnp.zeros_like(o_ref)

    nrow = te // PEER_NKEYS
    base = (pl.program_id(1) * nrow) % ROW_TILE
    hid = lax.dot_general(u_ref[...], xn_ref[...], NT_DIMS, preferred_element_type=F32)
    zero = jnp.zeros((), g_ref.dtype)
    for ii in range(nrow):
        acc = None
        for h in range(nheads):
            c1row = c1_ref[h, pl.ds(base + ii, 1), :].astype(g_ref.dtype)
            e1row = e1_ref[h, pl.ds(base + ii, 1), :].astype(g_ref.dtype)
            term = jnp.where(r2_ref[h] < c1row, e2_ref[h] * e1row, zero)
            acc = term if acc is None else acc + term
        g_ref[ii * PEER_NKEYS:(ii + 1) * PEER_NKEYS, :] = acc
    act = (0.5 * hid * (1.0 + lax.erf(hid * INV_SQRT2)) * g_ref[...].astype(F32)).astype(BF16)
    o_ref[...] += lax.dot_general(act, v_ref[...], TN_DIMS, preferred_element_type=F32)


def _peer_dense(xn, u, v, c1, e1, r2, e2):
    t, d = xn.shape
    ne = u.shape[0]
    nheads = c1.shape[0]
    tt = _pick(t, (512, 256, 128))
    te = _pick(ne, (512, 256, 128))
    nrow = te // PEER_NKEYS
    big = pl.BlockSpec((nheads, PEER_NKEYS, tt), lambda i, e: (0, 0, i))
    assert ROW_TILE % nrow == 0
    rows = pl.BlockSpec((nheads, ROW_TILE, tt), lambda i, e: (0, (e * nrow) // ROW_TILE, i))
    return pl.pallas_call(
        functools.partial(_peer_body, nheads=nheads, te=te),
        grid=(t // tt, ne // te),
        in_specs=[pl.BlockSpec((tt, d), lambda i, e: (i, 0)),
                  pl.BlockSpec((te, d), lambda i, e: (e, 0)),
                  pl.BlockSpec((te, d), lambda i, e: (e, 0)),
                  rows, rows, big, big],
        out_specs=pl.BlockSpec((tt, d), lambda i, e: (i, 0)),
        out_shape=jax.ShapeDtypeStruct((t, d), F32),
        scratch_shapes=[pltpu.VMEM((te, tt), BF16)],
        compiler_params=_params(2),
        name="peer_dense",
    )(xn, u, v, c1, e1, r2, e2)


def _add_rms_body(a_ref, b_ref, w_ref, s_ref, n_ref):
    x = a_ref[...] + b_ref[...]
    s_ref[...] = x
    ms = jnp.mean(x * x, axis=-1, keepdims=True)
    n_ref[...] = (x * lax.rsqrt(ms + EPS) * w_ref[...]).astype(n_ref.dtype)


def _add_rmsnorm(a, b, w):
    t, d = a.shape
    tm = _pick(t, (256, 128, 64, 8))
    spec = pl.BlockSpec((tm, d), lambda i: (i, 0))
    return pl.pallas_call(
        _add_rms_body,
        grid=(t // tm,),
        in_specs=[spec, spec, pl.BlockSpec((1, d), lambda i: (0, 0))],
        out_specs=[spec, spec],
        out_shape=[jax.ShapeDtypeStruct((t, d), F32), jax.ShapeDtypeStruct((t, d), BF16)],
        compiler_params=_params(1),
        name="add_rmsnorm",
    )(a, b, w.reshape(1, d))


def _ple_body(xn_ref, wg_ref, p_ref, wp_ref, h_ref, pw_ref, fw_ref, o_ref, emb_ref, *, tn):
    j = pl.program_id(1)

    @pl.when(j == 0)
    def _():
        raw = jnp.dot(p_ref[...], wp_ref[...], preferred_element_type=F32)
        ms = jnp.mean(raw * raw, axis=-1, keepdims=True)
        emb_ref[...] = raw * lax.rsqrt(ms + EPS) * pw_ref[...]

    cs = pl.ds(pl.multiple_of(j * tn, tn), tn)
    gate = _sigmoid(jnp.dot(xn_ref[...], wg_ref[...], preferred_element_type=F32))
    o_ref[:, cs] = h_ref[...] + gate * emb_ref[:, cs]

    @pl.when(j == pl.num_programs(1) - 1)
    def _():
        x = o_ref[...]
        ms = jnp.mean(x * x, axis=-1, keepdims=True)
        o_ref[...] = x * lax.rsqrt(ms + EPS) * fw_ref[...]


def _ple_final(xn, w_gate, p, w_proj, h, post_w, final_w):
    t, d = h.shape
    pd = p.shape[1]
    tm = _pick(t, (512, 256, 128, 64, 8))
    tn = _pick(d, (512, 256, 128))
    return pl.pallas_call(
        functools.partial(_ple_body, tn=tn),
        grid=(t // tm, d // tn),
        in_specs=[pl.BlockSpec((tm, d), lambda i, j: (i, 0)),
                  pl.BlockSpec((d, tn), lambda i, j: (0, j)),
                  pl.BlockSpec((tm, pd), lambda i, j: (i, 0)),
                  pl.BlockSpec((pd, d), lambda i, j: (0, 0)),
                  pl.BlockSpec((tm, tn), lambda i, j: (i, j)),
                  pl.BlockSpec((1, d), lambda i, j: (0, 0)),
                  pl.BlockSpec((1, d), lambda i, j: (0, 0))],
        out_specs=pl.BlockSpec((tm, d), lambda i, j: (i, 0)),
        out_shape=jax.ShapeDtypeStruct((t, d), F32),
        scratch_shapes=[pltpu.VMEM((tm, d), F32)],
        compiler_params=_params(2),
        name="ple_final",
    )(xn, w_gate, p, w_proj, h, post_w.reshape(1, d), final_w.reshape(1, d))


def kernel(x, p, attn_norm_w, w_in, hg_lower_bound, gd_conv_w, gd_A_log, gd_dt_bias, hg_out_norm_w, gd_out_norm_w, w_out, ffn_norm_w, peer_w_query, peer_sub_keys, peer_u, peer_v, ple_norm_w, ple_w_gate, ple_w_proj, ple_post_norm_w, final_norm_w):
    bsz, seq, d = x.shape
    depth = w_in.shape[0]
    assert bsz == 1 and depth == 1
    w = d // 2
    nh = w // HEAD_DIM
    assert 4 * nh <= LANES
    t = bsz * seq
    h = x.reshape(t, d)
    i = 0

    n_main = 8 * w
    wi = w_in[i].astype(BF16)
    xn = _rmsnorm(h, attn_norm_w[i])
    proj = _matmul(xn, wi, name="in_proj", ncols=n_main)
    gab = _matmul(xn, wi, name="in_proj_gates", col0=n_main, ncols=LANES)
    gz = _matmul(xn, w_in[i][:, n_main + 4 * nh:].astype(BF16), name="in_proj_gz")

    oa = []
    for dr, rev in ((0, False), (1, True)):
        lbp = hg_lower_bound[:, dr, :].reshape(depth + 1, 1, w)
        oa.append(_hgrn_scan(proj, lbp, nh=nh, gq=0, gf=1 + dr, gi=3, reverse=rev))

    qkv = _conv_qkv(proj, gd_conv_w[i], nh=nh, g0=5)
    pad = jnp.zeros((LANES - 2 * nh,), F32)
    arow = jnp.concatenate([gd_A_log[i].reshape(-1), pad]).reshape(1, LANES)
    dtrow = jnp.concatenate([gd_dt_bias[i].reshape(-1), pad]).reshape(1, LANES)
    ob = [_gdn_scan(qkv, gab, arow, dtrow, nh=nh, reverse=rev) for rev in (False, True)]

    mix_in = _gated_norm(oa[0], oa[1], ob[0], ob[1], proj, gz, hg_out_norm_w[i], gd_out_norm_w[i],
                         nh=nh, ghz=4)
    h1 = _matmul(mix_in, w_out[i].astype(BF16), residual=h, name="out_proj")

    xn2 = _rmsnorm(h1, ffn_norm_w[i])
    qry = _matmul(xn2, peer_w_query[i].astype(BF16), name="peer_query")
    c1, e1, r2, e2 = _peer_route(qry, peer_sub_keys[i])
    peer_out = _peer_dense(xn2, peer_u[i].astype(BF16), peer_v[i].astype(BF16), c1, e1, r2, e2)
    h2, xn3 = _add_rmsnorm(h1, peer_out, ple_norm_w[i])

    out = _ple_final(xn3, ple_w_gate[i].astype(BF16), p[i].reshape(t, -1).astype(BF16),
                     ple_w_proj[i].astype(BF16), h2, ple_post_norm_w[i], final_norm_w)
    return out.reshape(bsz, seq, d)
```

```python
import functools

import jax
import jax.numpy as jnp
from jax import lax
from jax.experimental import pallas as pl
from jax.experimental.pallas import tpu as pltpu

F32 = jnp.float32
BF16 = jnp.bfloat16
EPS = 1e-6
HEAD_DIM = 128
CHUNK = 64
SUB = 8
HGRN_GROUP = 8
PEER_TOPK = 16
PEER_NKEYS = 128
LANES = 128
ROW_TILE = 8
VMEM_LIMIT_BYTES = 56 * 1024 * 1024
HIGHEST = lax.Precision.HIGHEST
NT_DIMS = (((1,), (1,)), ((), ()))
TN_DIMS = (((0,), (0,)), ((), ()))
INV_SQRT2 = 0.7071067811865476


def _params(n_axes, **kw):
    return pltpu.CompilerParams(dimension_semantics=("arbitrary",) * n_axes,
                                vmem_limit_bytes=VMEM_LIMIT_BYTES, **kw)


def _sigmoid(x):
    return 1.0 / (1.0 + jnp.exp(-x))


def _pick(n, prefs):
    for t in prefs:
        if n % t == 0:
            return t
    return n


def _rms_body(x_ref, w_ref, o_ref):
    x = x_ref[...]
    ms = jnp.mean(x * x, axis=-1, keepdims=True)
    o_ref[...] = (x * lax.rsqrt(ms + EPS) * w_ref[...]).astype(o_ref.dtype)


def _rmsnorm(x, w):
    t, d = x.shape
    tm = _pick(t, (512, 256, 128, 64, 8))
    return pl.pallas_call(
        _rms_body,
        grid=(t // tm,),
        in_specs=[pl.BlockSpec((tm, d), lambda i: (i, 0)),
                  pl.BlockSpec((1, d), lambda i: (0, 0))],
        out_specs=pl.BlockSpec((tm, d), lambda i: (i, 0)),
        out_shape=jax.ShapeDtypeStruct((t, d), BF16),
        compiler_params=_params(1),
        name="rmsnorm",
    )(x, w.reshape(1, d))


def _cast_body(x_ref, o_ref):
    o_ref[...] = x_ref[0].astype(o_ref.dtype)


def _to_bf16(w3, layer):
    _, m, n = w3.shape
    tm = _pick(m, (512, 256, 128, 64, 8))
    return pl.pallas_call(
        _cast_body,
        grid=(m // tm,),
        in_specs=[pl.BlockSpec((1, tm, n), lambda i: (layer, i, 0))],
        out_specs=pl.BlockSpec((tm, n), lambda i: (i, 0)),
        out_shape=jax.ShapeDtypeStruct((m, n), BF16),
        compiler_params=_params(1),
        name="to_bf16",
    )(w3)


def _mm_body(a_ref, b_ref, o_ref):
    o_ref[...] = jnp.dot(a_ref[...], b_ref[...], preferred_element_type=F32)


def _mm_res_body(a_ref, b_ref, r_ref, o_ref):
    o_ref[...] = r_ref[...] + jnp.dot(a_ref[...], b_ref[...], preferred_element_type=F32)


def _matmul(a, b, residual=None, name="matmul", col0=0, ncols=None):
    m, k = a.shape
    n = b.shape[1] if ncols is None else ncols
    tm = _pick(m, (1024, 512, 256, 128, 64, 8))
    tn = _pick(n, (512, 256, 128))
    assert col0 % tn == 0
    jb = col0 // tn
    in_specs = [pl.BlockSpec((tm, k), lambda i, j: (i, 0)),
                pl.BlockSpec((k, tn), lambda i, j: (0, jb + j))]
    args = [a, b]
    body = _mm_body
    if residual is not None:
        in_specs.append(pl.BlockSpec((tm, tn), lambda i, j: (i, j)))
        args.append(residual)
        body = _mm_res_body
    return pl.pallas_call(
        body,
        grid=(m // tm, n // tn),
        in_specs=in_specs,
        out_specs=pl.BlockSpec((tm, tn), lambda i, j: (i, j)),
        out_shape=jax.ShapeDtypeStruct((m, n), F32),
        compiler_params=_params(2),
        name=name,
    )(*args)


def _tri(reverse, strict=False):
    r = lax.broadcasted_iota(jnp.int32, (CHUNK, CHUNK), 0)
    c = lax.broadcasted_iota(jnp.int32, (CHUNK, CHUNK), 1)
    if reverse:
        return (c > r) if strict else (c >= r)
    return (c < r) if strict else (c <= r)


def _hgrn_body(hq_ref, hf_ref, hi_ref, lbp_ref, o_ref, st_ref, q_sc, k_sc, b_sc, *, nh, reverse):
    @pl.when(pl.program_id(0) == 0)
    def _():
        st_ref[...] = jnp.zeros_like(st_ref)

    nl = lbp_ref.shape[0]
    mx = lbp_ref[0]
    for l in range(1, nl):
        mx = jnp.maximum(mx, lbp_ref[l])
    den = jnp.exp(lbp_ref[0] - mx)
    e0 = den
    for l in range(1, nl):
        den = den + jnp.exp(lbp_ref[l] - mx)
    lb = e0 / den

    zq = hq_ref[...]
    s = _sigmoid(hf_ref[...])
    q_sc[...] = zq * _sigmoid(zq) * (HEAD_DIM ** -0.5)
    k_sc[...] = (1.0 - lb) * (1.0 - s)
    logf = jnp.log2(lb + (1.0 - lb) * s)
    b_sc[...] = jnp.dot(_tri(reverse).astype(F32), logf, precision=HIGHEST, preferred_element_type=F32)

    row_c = lax.broadcasted_iota(jnp.int32, (CHUNK, CHUNK), 0)
    col_c = lax.broadcasted_iota(jnp.int32, (CHUNK, CHUNK), 1)
    row8 = lax.broadcasted_iota(jnp.int32, (SUB, 1), 0)
    nsub = CHUNK // SUB
    last = 0 if reverse else CHUNK - 1

    for g0 in range(0, nh, HGRN_GROUP):
        heads = list(range(g0, min(g0 + HGRN_GROUP, nh)))
        sls = {h: slice(h * HEAD_DIM, (h + 1) * HEAD_DIM) for h in heads}
        stb = {h: st_ref[h].astype(BF16) for h in heads}
        o = {h: lax.dot_general((q_sc[:, sls[h]] * jnp.exp2(b_sc[:, sls[h]])).astype(BF16), stb[h], NT_DIMS,
                                preferred_element_type=F32) for h in heads}
        cross = {h: None for h in heads}
        bs = CHUNK
        while bs >= 2 * SUB:
            half = bs // 2
            same = (row_c // bs) == (col_c // bs)
            for h in heads:
                qparts, kparts = [], []
                for s0 in range(0, CHUNK, bs):
                    lo_rows, hi_rows = slice(s0, s0 + half), slice(s0 + half, s0 + bs)
                    qs, ks, rr = (lo_rows, hi_rows, s0 + half) if reverse else (hi_rows, lo_rows, s0 + half - 1)
                    ref = b_sc[rr:rr + 1, sls[h]]
                    qpart = q_sc[qs, sls[h]] * jnp.exp2(b_sc[qs, sls[h]] - ref)
                    kpart = k_sc[ks, sls[h]] * jnp.exp2(ref - b_sc[ks, sls[h]])
                    zero = jnp.zeros_like(qpart)
                    qparts += [qpart, zero] if reverse else [zero, qpart]
                    kparts += [zero, kpart] if reverse else [kpart, zero]
                a = lax.dot_general(jnp.concatenate(qparts, axis=0).astype(BF16),
                                    jnp.concatenate(kparts, axis=0).astype(BF16), NT_DIMS,
                                    preferred_element_type=F32)
                if bs < CHUNK:
                    a = jnp.where(same, a, 0.0)
                cross[h] = a if cross[h] is None else cross[h] + a
            bs = half
        for h in heads:
            o[h] = o[h] + jnp.dot(cross[h].astype(BF16), hi_ref[:, sls[h]].astype(BF16),
                                  preferred_element_type=F32)
        for h in heads:
            diag = []
            for j in range(nsub):
                rs = slice(SUB * j, SUB * (j + 1))
                qj = q_sc[rs, sls[h]]
                bj = b_sc[rs, sls[h]]
                oj = jnp.zeros((SUB, HEAD_DIM), F32)
                for s_loc in range(SUB):
                    r = SUB * j + s_loc
                    pr = qj * k_sc[r:r + 1, sls[h]] * jnp.exp2(bj - b_sc[r:r + 1, sls[h]])
                    a = jnp.sum(pr, axis=1, keepdims=True)
                    keep = (row8 <= s_loc) if reverse else (row8 >= s_loc)
                    oj = oj + jnp.where(keep, a, 0.0) * hi_ref[r:r + 1, sls[h]]
                diag.append(oj)
            o_ref[:, sls[h]] = o[h] + jnp.concatenate(diag, axis=0)
        for h in heads:
            b_last = b_sc[last:last + 1, sls[h]]
            kend = (k_sc[:, sls[h]] * jnp.exp2(b_last - b_sc[:, sls[h]])).astype(BF16)
            st_ref[h] = st_ref[h] * jnp.exp2(b_last) + lax.dot_general(
                hi_ref[:, sls[h]].astype(BF16), kend, TN_DIMS, preferred_element_type=F32)


def _hgrn_scan(proj, lbp, *, nh, gq, gf, gi, reverse):
    t = proj.shape[0]
    w = nh * HEAD_DIM
    nc = t // CHUNK
    cidx = (lambda i: nc - 1 - i) if reverse else (lambda i: i)
    spec = lambda g: pl.BlockSpec((CHUNK, w), lambda i: (cidx(i), g))
    return pl.pallas_call(
        functools.partial(_hgrn_body, nh=nh, reverse=reverse),
        grid=(nc,),
        in_specs=[spec(gq), spec(gf), spec(gi),
                  pl.BlockSpec(lbp.shape, lambda i: (0, 0, 0))],
        out_specs=pl.BlockSpec((CHUNK, w), lambda i: (cidx(i), 0)),
        out_shape=jax.ShapeDtypeStruct((t, w), F32),
        scratch_shapes=[pltpu.VMEM((nh, HEAD_DIM, HEAD_DIM), F32)] + [pltpu.VMEM((CHUNK, w), F32)] * 3,
        compiler_params=_params(1),
        name="hgrn2_bw" if reverse else "hgrn2_fw",
    )(proj, proj, proj, lbp)


HALO = 8


def _conv_body(x_ref, xp_ref, xn_ref, w_ref, o_ref, xe_ref, *, nh, tt, kw):
    i = pl.program_id(0)
    g = pl.program_id(1)
    n = pl.num_programs(0)
    xe_ref[0:HALO] = jnp.where(i > 0, xp_ref[...], 0.0)
    xe_ref[HALO:HALO + tt] = x_ref[...]
    xe_ref[HALO + tt:HALO + tt + HALO] = jnp.where(i < n - 1, xn_ref[...], 0.0)
    pad = (kw - 1) // 2
    acc = None
    for j in range(kw):
        term = w_ref[j:j + 1] * xe_ref[HALO - pad + j:HALO - pad + j + tt]
        acc = term if acc is None else acc + term
    y = acc * _sigmoid(acc)
    for h in range(nh):
        sl = slice(h * HEAD_DIM, (h + 1) * HEAD_DIM)
        yh = y[:, sl]
        inv = lax.rsqrt(jnp.sum(yh * yh, axis=1, keepdims=True) + EPS)
        scale = jnp.where(g == 0, inv * (HEAD_DIM ** -0.5), jnp.where(g == 1, inv, 1.0))
        o_ref[0, :, sl] = yh * scale


def _conv_qkv(proj, conv_w, *, nh, g0):
    t = proj.shape[0]
    w = nh * HEAD_DIM
    kw = conv_w.shape[0]
    tt = _pick(t, (256, 128, 64))
    nb = tt // HALO
    last = t // HALO - 1
    return pl.pallas_call(
        functools.partial(_conv_body, nh=nh, tt=tt, kw=kw),
        grid=(t // tt, 3),
        in_specs=[pl.BlockSpec((tt, w), lambda i, g: (i, g0 + g)),
                  pl.BlockSpec((HALO, w), lambda i, g: (jnp.maximum(i * nb - 1, 0), g0 + g)),
                  pl.BlockSpec((HALO, w), lambda i, g: (jnp.minimum((i + 1) * nb, last), g0 + g)),
                  pl.BlockSpec((kw, w), lambda i, g: (0, g))],
        out_specs=pl.BlockSpec((1, tt, w), lambda i, g: (g, i, 0)),
        out_shape=jax.ShapeDtypeStruct((3, t, w), F32),
        scratch_shapes=[pltpu.VMEM((tt + 2 * HALO, w), F32)],
        compiler_params=_params(2),
        name="gdn_conv",
    )(proj, proj, proj, conv_w)


def _gdn_body(q_ref, k_ref, v_ref, gab_ref, arow_ref, dtrow_ref, o_ref, st_ref, *, nh, reverse):
    @pl.when(pl.program_id(0) == 0)
    def _():
        st_ref[...] = jnp.zeros_like(st_ref)

    gab = gab_ref[...]
    xs = gab + dtrow_ref[...]
    softplus = jnp.maximum(xs, 0.0) + jnp.log(1.0 + jnp.exp(-jnp.abs(xs)))
    dec = -jnp.exp(arow_ref[...]) * softplus
    beta_all = _sigmoid(gab)
    cum = _tri(reverse).astype(F32)
    gam_all = jnp.dot(cum, dec, precision=HIGHEST, preferred_element_type=F32)
    gam_t = gam_all.T
    incl = _tri(reverse)
    strict = _tri(reverse, strict=True)
    r = lax.broadcasted_iota(jnp.int32, (CHUNK, CHUNK), 0)
    c = lax.broadcasted_iota(jnp.int32, (CHUNK, CHUNK), 1)
    eye = (r == c).astype(F32)
    last = 0 if reverse else CHUNK - 1
    d = 1 if reverse else 0

    heads = range(nh)
    sls = [slice(h * HEAD_DIM, (h + 1) * HEAD_DIM) for h in heads]
    gam_c = [gam_all[:, d * nh + h:d * nh + h + 1] for h in heads]
    beta_c = [beta_all[:, 2 * nh + d * nh + h:2 * nh + d * nh + h + 1] for h in heads]
    rel = [jnp.where(incl, jnp.exp(gam_c[h] - gam_t[d * nh + h:d * nh + h + 1, :]), 0.0) for h in heads]
    kb = [k_ref[0, :, sls[h]].astype(BF16) for h in heads]
    kk = [lax.dot_general(kb[h], kb[h], NT_DIMS, preferred_element_type=F32) for h in heads]
    qk = [lax.dot_general(q_ref[0, :, sls[h]].astype(BF16), kb[h], NT_DIMS, preferred_element_type=F32)
          for h in heads]
    qk = [jnp.where(incl, qk[h] * rel[h], 0.0).astype(BF16) for h in heads]
    m = [jnp.where(strict, -(beta_c[h] * kk[h] * rel[h]), 0.0) for h in heads]
    tinv = [eye + m[h] for h in heads]
    pw = [m[h].astype(BF16) for h in heads]
    sq = 1
    while 2 * sq < CHUNK:
        pw = [jnp.dot(pw[h], pw[h], preferred_element_type=F32).astype(BF16) for h in heads]
        tinv = [tinv[h] + jnp.dot(tinv[h].astype(BF16), pw[h], preferred_element_type=F32) for h in heads]
        sq *= 2
    sol = []
    for h in heads:
        k = k_ref[0, :, sls[h]]
        rhs = jnp.concatenate([v_ref[0, :, sls[h]] * beta_c[h], k * (beta_c[h] * jnp.exp(gam_c[h]))], axis=1)
        sol.append(jnp.dot(tinv[h].astype(BF16), rhs.astype(BF16), preferred_element_type=F32))
    stb = [st_ref[h].astype(BF16) for h in heads]
    vnb = [(sol[h][:, :HEAD_DIM]
            - lax.dot_general(sol[h][:, HEAD_DIM:].astype(BF16), stb[h], NT_DIMS, preferred_element_type=F32)
            ).astype(BF16) for h in heads]
    for h in heads:
        qdec = (q_ref[0, :, sls[h]] * jnp.exp(gam_c[h])).astype(BF16)
        o_ref[:, sls[h]] = (lax.dot_general(qdec, stb[h], NT_DIMS, preferred_element_type=F32)
                            + jnp.dot(qk[h], vnb[h], preferred_element_type=F32))
    for h in heads:
        gam_last = gam_c[h][last:last + 1]
        kdec = (k_ref[0, :, sls[h]] * jnp.exp(gam_last - gam_c[h])).astype(BF16)
        st_ref[h] = st_ref[h] * jnp.exp(gam_last) + lax.dot_general(vnb[h], kdec, TN_DIMS,
                                                                    preferred_element_type=F32)


def _gdn_scan(qkv, gab, arow, dtrow, *, nh, reverse):
    t = qkv.shape[1]
    w = nh * HEAD_DIM
    nc = t // CHUNK
    cidx = (lambda i: nc - 1 - i) if reverse else (lambda i: i)
    spec = lambda g: pl.BlockSpec((1, CHUNK, w), lambda i: (g, cidx(i), 0))
    return pl.pallas_call(
        functools.partial(_gdn_body, nh=nh, reverse=reverse),
        grid=(nc,),
        in_specs=[spec(0), spec(1), spec(2),
                  pl.BlockSpec((CHUNK, LANES), lambda i: (cidx(i), 0)),
                  pl.BlockSpec((1, LANES), lambda i: (0, 0)),
                  pl.BlockSpec((1, LANES), lambda i: (0, 0))],
        out_specs=pl.BlockSpec((CHUNK, w), lambda i: (cidx(i), 0)),
        out_shape=jax.ShapeDtypeStruct((t, w), F32),
        scratch_shapes=[pltpu.VMEM((nh, HEAD_DIM, HEAD_DIM), F32)],
        compiler_params=_params(1),
        name="gdn_bw" if reverse else "gdn_fw",
    )(qkv, qkv, qkv, gab, arow, dtrow)


def _gnorm_body(af_ref, ab_ref, bf_ref, bb_ref, hz_ref, gz_ref, wa_ref, wb_ref, o_ref, *, nh):
    w = nh * HEAD_DIM
    for grp, (f_ref, b_ref, z_ref, w_ref) in enumerate(((af_ref, ab_ref, hz_ref, wa_ref),
                                                        (bf_ref, bb_ref, gz_ref, wb_ref))):
        for h in range(nh):
            sl = slice(h * HEAD_DIM, (h + 1) * HEAD_DIM)
            o = f_ref[:, sl] + b_ref[:, sl]
            y = o * lax.rsqrt(jnp.mean(o * o, axis=1, keepdims=True) + EPS) * w_ref[...]
            z = z_ref[:, sl]
            osl = slice(grp * w + h * HEAD_DIM, grp * w + (h + 1) * HEAD_DIM)
            o_ref[:, osl] = (y * (z * _sigmoid(z))).astype(o_ref.dtype)


def _gated_norm(oa_f, oa_b, ob_f, ob_b, proj, gz, wa, wb, *, nh, ghz):
    t = oa_f.shape[0]
    w = nh * HEAD_DIM
    tm = _pick(t, (256, 128, 64))
    spec = pl.BlockSpec((tm, w), lambda i: (i, 0))
    return pl.pallas_call(
        functools.partial(_gnorm_body, nh=nh),
        grid=(t // tm,),
        in_specs=[spec, spec, spec, spec,
                  pl.BlockSpec((tm, w), lambda i: (i, ghz)),
                  spec,
                  pl.BlockSpec((1, HEAD_DIM), lambda i: (0, 0)),
                  pl.BlockSpec((1, HEAD_DIM), lambda i: (0, 0))],
        out_specs=pl.BlockSpec((tm, 2 * w), lambda i: (i, 0)),
        out_shape=jax.ShapeDtypeStruct((t, 2 * w), BF16),
        compiler_params=_params(1),
        name="gated_head_norm",
    )(oa_f, oa_b, ob_f, ob_b, proj, gz, wa.reshape(1, HEAD_DIM), wb.reshape(1, HEAD_DIM))


def _cand_rows():
    out = []
    for a in range(PEER_TOPK):
        out.append((a, min(PEER_TOPK, PEER_TOPK // (a + 1))))
    return out


N_CAND = sum(nb for _, nb in _cand_rows())
N_CAND_PAD = -(-N_CAND // 8) * 8


def _route_body(qry_ref, sk_ref, c1_ref, e1_ref, r2_ref, e2_ref, top_ref, cand_ref, *, nheads):
    ninf = -jnp.inf
    for h in range(nheads):
        scores, ranks = [], []
        for p in range(2):
            j = 2 * h + p
            qhp = qry_ref[:, j * PEER_NKEYS:(j + 1) * PEER_NKEYS].astype(BF16)
            keys = sk_ref[h, p].astype(BF16)
            s = lax.dot_general(keys, qhp, NT_DIMS, preferred_element_type=F32)
            scores.append(s)
            wv = s
            rk = jnp.full(s.shape, float(PEER_NKEYS), F32)
            for rnk in range(PEER_TOPK):
                mx = jnp.max(wv, axis=0, keepdims=True)
                top_ref[p * PEER_TOPK + rnk:p * PEER_TOPK + rnk + 1, :] = mx
                sel = wv == mx
                if p == 1:
                    rk = jnp.where(sel, float(rnk), rk)
                wv = jnp.where(sel, ninf, wv)
            ranks.append(rk)
        t1 = top_ref[0:PEER_TOPK]
        t2 = top_ref[PEER_TOPK:2 * PEER_TOPK]
        cand_ref[...] = jnp.full(cand_ref.shape, ninf, F32)
        off = 0
        for a, nb in _cand_rows():
            cand_ref[off:off + nb] = t1[a:a + 1] + t2[0:nb]
            off += nb
        cnd = cand_ref[...]
        cmax = t1[0:1] + t2[0:1]
        z = jnp.zeros_like(cmax)
        mx = cmax
        for rnk in range(PEER_TOPK):
            mx = jnp.max(cnd, axis=0, keepdims=True)
            z = z + jnp.exp(mx - cmax)
            cnd = jnp.where(cnd == mx, ninf, cnd)
        s1, s2 = scores
        c1 = jnp.zeros_like(s1)
        for a in range(PEER_TOPK):
            cnt = jnp.sum(jnp.where((t1[a:a + 1] + t2) >= mx, 1.0, 0.0), axis=0, keepdims=True)
            c1 = jnp.where(s1 == t1[a:a + 1], cnt, c1)
        c1_ref[h] = c1
        e1_ref[h] = jnp.exp(s1 - t1[0:1]) / z
        r2_ref[h] = ranks[1].astype(r2_ref.dtype)
        e2_ref[h] = jnp.exp(s2 - t2[0:1]).astype(e2_ref.dtype)


def _peer_route(qry, sub_keys):
    t = qry.shape[0]
    nheads = sub_keys.shape[0]
    tt = _pick(t, (512, 256, 128))
    big = pl.BlockSpec((nheads, PEER_NKEYS, tt), lambda i: (0, 0, i))
    f32_shape = jax.ShapeDtypeStruct((nheads, PEER_NKEYS, t), F32)
    b16_shape = jax.ShapeDtypeStruct((nheads, PEER_NKEYS, t), BF16)
    return pl.pallas_call(
        functools.partial(_route_body, nheads=nheads),
        grid=(t // tt,),
        in_specs=[pl.BlockSpec((tt, qry.shape[1]), lambda i: (i, 0)),
                  pl.BlockSpec(sub_keys.shape, lambda i: (0, 0, 0, 0))],
        out_specs=[big, big, big, big],
        out_shape=[f32_shape, f32_shape, b16_shape, b16_shape],
        scratch_shapes=[pltpu.VMEM((2 * PEER_TOPK, tt), F32), pltpu.VMEM((N_CAND_PAD, tt), F32)],
        compiler_params=_params(1),
        name="peer_route",
    )(qry, sub_keys)


def _peer_body(xn_ref, u_ref, v_ref, c1_ref, e1_ref, r2_ref, e2_ref, o_ref, g_ref, *, nheads, te):
    @pl.when(pl.program_id(1) == 0)
    def _():
        o_ref[...] = jnp.zeros_like(o_ref)

    nrow = te // PEER_NKEYS
    base = (pl.program_id(1) * nrow) % ROW_TILE
    hid = lax.dot_general(u_ref[...], xn_ref[...], NT_DIMS, preferred_element_type=F32)
    zero = jnp.zeros((), g_ref.dtype)
    for ii in range(nrow):
        acc = None
        for h in range(nheads):
            c1row = c1_ref[h, pl.ds(base + ii, 1), :].astype(g_ref.dtype)
            e1row = e1_ref[h, pl.ds(base + ii, 1), :].astype(g_ref.dtype)
            term = jnp.where(r2_ref[h] < c1row, e2_ref[h] * e1row, zero)
            acc = term if acc is None else acc + term
        g_ref[ii * PEER_NKEYS:(ii + 1) * PEER_NKEYS, :] = acc
    act = (0.5 * hid * (1.0 + lax.erf(hid * INV_SQRT2)) * g_ref[...].astype(F32)).astype(BF16)
    o_ref[...] += lax.dot_general(act, v_ref[...], TN_DIMS, preferred_element_type=F32)


def _peer_dense(xn, u, v, c1, e1, r2, e2):
    t, d = xn.shape
    ne = u.shape[0]
    nheads = c1.shape[0]
    tt = _pick(t, (512, 256, 128))
    te = _pick(ne, (512, 256, 128))
    nrow = te // PEER_NKEYS
    big = pl.BlockSpec((nheads, PEER_NKEYS, tt), lambda i, e: (0, 0, i))
    assert ROW_TILE % nrow == 0
    rows = pl.BlockSpec((nheads, ROW_TILE, tt), lambda i, e: (0, (e * nrow) // ROW_TILE, i))
    return pl.pallas_call(
        functools.partial(_peer_body, nheads=nheads, te=te),
        grid=(t // tt, ne // te),
        in_specs=[pl.BlockSpec((tt, d), lambda i, e: (i, 0)),
                  pl.BlockSpec((te, d), lambda i, e: (e, 0)),
                  pl.BlockSpec((te, d), lambda i, e: (e, 0)),
                  rows, rows, big, big],
        out_specs=pl.BlockSpec((tt, d), lambda i, e: (i, 0)),
        out_shape=jax.ShapeDtypeStruct((t, d), F32),
        scratch_shapes=[pltpu.VMEM((te, tt), BF16)],
        compiler_params=_params(2),
        name="peer_dense",
    )(xn, u, v, c1, e1, r2, e2)


def _add_rms_body(a_ref, b_ref, w_ref, s_ref, n_ref):
    x = a_ref[...] + b_ref[...]
    s_ref[...] = x
    ms = jnp.mean(x * x, axis=-1, keepdims=True)
    n_ref[...] = (x * lax.rsqrt(ms + EPS) * w_ref[...]).astype(n_ref.dtype)


def _add_rmsnorm(a, b, w):
    t, d = a.shape
    tm = _pick(t, (256, 128, 64, 8))
    spec = pl.BlockSpec((tm, d), lambda i: (i, 0))
    return pl.pallas_call(
        _add_rms_body,
        grid=(t // tm,),
        in_specs=[spec, spec, pl.BlockSpec((1, d), lambda i: (0, 0))],
        out_specs=[spec, spec],
        out_shape=[jax.ShapeDtypeStruct((t, d), F32), jax.ShapeDtypeStruct((t, d), BF16)],
        compiler_params=_params(1),
        name="add_rmsnorm",
    )(a, b, w.reshape(1, d))


def _ple_body(xn_ref, wg_ref, p_ref, wp_ref, h_ref, pw_ref, fw_ref, o_ref, emb_ref, *, tn):
    j = pl.program_id(1)

    @pl.when(j == 0)
    def _():
        raw = jnp.dot(p_ref[...], wp_ref[...], preferred_element_type=F32)
        ms = jnp.mean(raw * raw, axis=-1, keepdims=True)
        emb_ref[...] = raw * lax.rsqrt(ms + EPS) * pw_ref[...]

    cs = pl.ds(pl.multiple_of(j * tn, tn), tn)
    gate = _sigmoid(jnp.dot(xn_ref[...], wg_ref[...], preferred_element_type=F32))
    o_ref[:, cs] = h_ref[...] + gate * emb_ref[:, cs]

    @pl.when(j == pl.num_programs(1) - 1)
    def _():
        x = o_ref[...]
        ms = jnp.mean(x * x, axis=-1, keepdims=True)
        o_ref[...] = x * lax.rsqrt(ms + EPS) * fw_ref[...]


def _ple_final(xn, w_gate, p, w_proj, h, post_w, final_w):
    t, d = h.shape
    pd = p.shape[1]
    tm = _pick(t, (512, 256, 128, 64, 8))
    tn = _pick(d, (512, 256, 128))
    return pl.pallas_call(
        functools.partial(_ple_body, tn=tn),
        grid=(t // tm, d // tn),
        in_specs=[pl.BlockSpec((tm, d), lambda i, j: (i, 0)),
                  pl.BlockSpec((d, tn), lambda i, j: (0, j)),
                  pl.BlockSpec((tm, pd), lambda i, j: (i, 0)),
                  pl.BlockSpec((pd, d), lambda i, j: (0, 0)),
                  pl.BlockSpec((tm, tn), lambda i, j: (i, j)),
                  pl.BlockSpec((1, d), lambda i, j: (0, 0)),
                  pl.BlockSpec((1, d), lambda i, j: (0, 0))],
        out_specs=pl.BlockSpec((tm, d), lambda i, j: (i, 0)),
        out_shape=jax.ShapeDtypeStruct((t, d), F32),
        scratch_shapes=[pltpu.VMEM((tm, d), F32)],
        compiler_params=_params(2),
        name="ple_final",
    )(xn, w_gate, p, w_proj, h, post_w.reshape(1, d), final_w.reshape(1, d))


def kernel(x, p, attn_norm_w, w_in, hg_lower_bound, gd_conv_w, gd_A_log, gd_dt_bias, hg_out_norm_w, gd_out_norm_w, w_out, ffn_norm_w, peer_w_query, peer_sub_keys, peer_u, peer_v, ple_norm_w, ple_w_gate, ple_w_proj, ple_post_norm_w, final_norm_w):
    bsz, seq, d = x.shape
    depth = w_in.shape[0]
    assert bsz == 1 and depth == 1
    w = d // 2
    nh = w // HEAD_DIM
    assert 4 * nh <= LANES
    t = bsz * seq
    h = x.reshape(t, d)
    i = 0

    n_main = 8 * w
    wi = w_in[i].astype(BF16)
    xn = _rmsnorm(h, attn_norm_w[i])
    proj = _matmul(xn, wi, name="in_proj", ncols=n_main)
    gab = _matmul(xn, wi, name="in_proj_gates", col0=n_main, ncols=LANES)
    gz = _matmul(xn, w_in[i][:, n_main + 4 * nh:].astype(BF16), name="in_proj_gz")

    oa = []
    for dr, rev in ((0, False), (1, True)):
        lbp = hg_lower_bound[:, dr, :].reshape(depth + 1, 1, w)
        oa.append(_hgrn_scan(proj, lbp, nh=nh, gq=0, gf=1 + dr, gi=3, reverse=rev))

    qkv = _conv_qkv(proj, gd_conv_w[i], nh=nh, g0=5)
    pad = jnp.zeros((LANES - 2 * nh,), F32)
    arow = jnp.concatenate([gd_A_log[i].reshape(-1), pad]).reshape(1, LANES)
    dtrow = jnp.concatenate([gd_dt_bias[i].reshape(-1), pad]).reshape(1, LANES)
    ob = [_gdn_scan(qkv, gab, arow, dtrow, nh=nh, reverse=rev) for rev in (False, True)]

    mix_in = _gated_norm(oa[0], oa[1], ob[0], ob[1], proj, gz, hg_out_norm_w[i], gd_out_norm_w[i],
                         nh=nh, ghz=4)
    h1 = _matmul(mix_in, w_out[i].astype(BF16), residual=h, name="out_proj")

    xn2 = _rmsnorm(h1, ffn_norm_w[i])
    qry = _matmul(xn2, peer_w_query[i].astype(BF16), name="peer_query")
    c1, e1, r2, e2 = _peer_route(qry, peer_sub_keys[i])
    peer_out = _peer_dense(xn2, _to_bf16(peer_u, i), _to_bf16(peer_v, i), c1, e1, r2, e2)
    h2, xn3 = _add_rmsnorm(h1, peer_out, ple_norm_w[i])

    out = _ple_final(xn3, ple_w_gate[i].astype(BF16), p[i].reshape(t, -1).astype(BF16),
                     ple_w_proj[i].astype(BF16), h2, ple_post_norm_w[i], final_norm_w)
    return out.reshape(bsz, seq, d)
```

```python
import functools

import jax
import jax.numpy as jnp
from jax import lax
from jax.experimental import pallas as pl
from jax.experimental.pallas import tpu as pltpu

F32 = jnp.float32
BF16 = jnp.bfloat16
EPS = 1e-6
HEAD_DIM = 128
CHUNK = 64
SUB = 8
HGRN_GROUP = 8
PEER_TOPK = 16
PEER_NKEYS = 128
LANES = 128
ROW_TILE = 8
VMEM_LIMIT_BYTES = 56 * 1024 * 1024
CAST_BLOCK_BYTES = 8 * 1024 * 1024
HIGHEST = lax.Precision.HIGHEST
NT_DIMS = (((1,), (1,)), ((), ()))
TN_DIMS = (((0,), (0,)), ((), ()))
INV_SQRT2 = 0.7071067811865476


def _params(n_axes, **kw):
    return pltpu.CompilerParams(dimension_semantics=("arbitrary",) * n_axes,
                                vmem_limit_bytes=VMEM_LIMIT_BYTES, **kw)


def _sigmoid(x):
    return 1.0 / (1.0 + jnp.exp(-x))


def _pick(n, prefs):
    for t in prefs:
        if n % t == 0:
            return t
    return n


def _rms_body(x_ref, w_ref, o_ref):
    x = x_ref[...]
    ms = jnp.mean(x * x, axis=-1, keepdims=True)
    o_ref[...] = (x * lax.rsqrt(ms + EPS) * w_ref[...]).astype(o_ref.dtype)


def _rmsnorm(x, w):
    t, d = x.shape
    tm = _pick(t, (512, 256, 128, 64, 8))
    return pl.pallas_call(
        _rms_body,
        grid=(t // tm,),
        in_specs=[pl.BlockSpec((tm, d), lambda i: (i, 0)),
                  pl.BlockSpec((1, d), lambda i: (0, 0))],
        out_specs=pl.BlockSpec((tm, d), lambda i: (i, 0)),
        out_shape=jax.ShapeDtypeStruct((t, d), BF16),
        compiler_params=_params(1),
        name="rmsnorm",
    )(x, w.reshape(1, d))


def _cast_body(x_ref, o_ref):
    o_ref[...] = x_ref[0].astype(o_ref.dtype)


def _to_bf16(w3, layer):
    _, m, n = w3.shape
    tm = _pick(m, tuple(c for c in (512, 256, 128, 64, 32, 16, 8) if c * n * 4 <= CAST_BLOCK_BYTES))
    return pl.pallas_call(
        _cast_body,
        grid=(m // tm,),
        in_specs=[pl.BlockSpec((1, tm, n), lambda i: (layer, i, 0))],
        out_specs=pl.BlockSpec((tm, n), lambda i: (i, 0)),
        out_shape=jax.ShapeDtypeStruct((m, n), BF16),
        compiler_params=_params(1),
        name="to_bf16",
    )(w3)


def _mm_body(a_ref, b_ref, o_ref):
    o_ref[...] = jnp.dot(a_ref[...], b_ref[...], preferred_element_type=F32)


def _mm_res_body(a_ref, b_ref, r_ref, o_ref):
    o_ref[...] = r_ref[...] + jnp.dot(a_ref[...], b_ref[...], preferred_element_type=F32)


def _matmul(a, b, residual=None, name="matmul", col0=0, ncols=None):
    m, k = a.shape
    n = b.shape[1] if ncols is None else ncols
    tm = _pick(m, (1024, 512, 256, 128, 64, 8))
    tn = _pick(n, (512, 256, 128))
    assert col0 % tn == 0
    jb = col0 // tn
    in_specs = [pl.BlockSpec((tm, k), lambda i, j: (i, 0)),
                pl.BlockSpec((k, tn), lambda i, j: (0, jb + j))]
    args = [a, b]
    body = _mm_body
    if residual is not None:
        in_specs.append(pl.BlockSpec((tm, tn), lambda i, j: (i, j)))
        args.append(residual)
        body = _mm_res_body
    return pl.pallas_call(
        body,
        grid=(m // tm, n // tn),
        in_specs=in_specs,
        out_specs=pl.BlockSpec((tm, tn), lambda i, j: (i, j)),
        out_shape=jax.ShapeDtypeStruct((m, n), F32),
        compiler_params=_params(2),
        name=name,
    )(*args)


def _tri(reverse, strict=False):
    r = lax.broadcasted_iota(jnp.int32, (CHUNK, CHUNK), 0)
    c = lax.broadcasted_iota(jnp.int32, (CHUNK, CHUNK), 1)
    if reverse:
        return (c > r) if strict else (c >= r)
    return (c < r) if strict else (c <= r)


def _hgrn_body(hq_ref, hf_ref, hi_ref, lbp_ref, o_ref, st_ref, q_sc, k_sc, b_sc, *, nh, reverse):
    @pl.when(pl.program_id(0) == 0)
    def _():
        st_ref[...] = jnp.zeros_like(st_ref)

    nl = lbp_ref.shape[0]
    mx = lbp_ref[0]
    for l in range(1, nl):
        mx = jnp.maximum(mx, lbp_ref[l])
    den = jnp.exp(lbp_ref[0] - mx)
    e0 = den
    for l in range(1, nl):
        den = den + jnp.exp(lbp_ref[l] - mx)
    lb = e0 / den

    zq = hq_ref[...]
    s = _sigmoid(hf_ref[...])
    q_sc[...] = zq * _sigmoid(zq) * (HEAD_DIM ** -0.5)
    k_sc[...] = (1.0 - lb) * (1.0 - s)
    logf = jnp.log2(lb + (1.0 - lb) * s)
    b_sc[...] = jnp.dot(_tri(reverse).astype(F32), logf, precision=HIGHEST, preferred_element_type=F32)

    row_c = lax.broadcasted_iota(jnp.int32, (CHUNK, CHUNK), 0)
    col_c = lax.broadcasted_iota(jnp.int32, (CHUNK, CHUNK), 1)
    row8 = lax.broadcasted_iota(jnp.int32, (SUB, 1), 0)
    nsub = CHUNK // SUB
    last = 0 if reverse else CHUNK - 1

    for g0 in range(0, nh, HGRN_GROUP):
        heads = list(range(g0, min(g0 + HGRN_GROUP, nh)))
        sls = {h: slice(h * HEAD_DIM, (h + 1) * HEAD_DIM) for h in heads}
        stb = {h: st_ref[h].astype(BF16) for h in heads}
        o = {h: lax.dot_general((q_sc[:, sls[h]] * jnp.exp2(b_sc[:, sls[h]])).astype(BF16), stb[h], NT_DIMS,
                                preferred_element_type=F32) for h in heads}
        cross = {h: None for h in heads}
        bs = CHUNK
        while bs >= 2 * SUB:
            half = bs // 2
            same = (row_c // bs) == (col_c // bs)
            for h in heads:
                qparts, kparts = [], []
                for s0 in range(0, CHUNK, bs):
                    lo_rows, hi_rows = slice(s0, s0 + half), slice(s0 + half, s0 + bs)
                    qs, ks, rr = (lo_rows, hi_rows, s0 + half) if reverse else (hi_rows, lo_rows, s0 + half - 1)
                    ref = b_sc[rr:rr + 1, sls[h]]
                    qpart = q_sc[qs, sls[h]] * jnp.exp2(b_sc[qs, sls[h]] - ref)
                    kpart = k_sc[ks, sls[h]] * jnp.exp2(ref - b_sc[ks, sls[h]])
                    zero = jnp.zeros_like(qpart)
                    qparts += [qpart, zero] if reverse else [zero, qpart]
                    kparts += [zero, kpart] if reverse else [kpart, zero]
                a = lax.dot_general(jnp.concatenate(qparts, axis=0).astype(BF16),
                                    jnp.concatenate(kparts, axis=0).astype(BF16), NT_DIMS,
                                    preferred_element_type=F32)
                if bs < CHUNK:
                    a = jnp.where(same, a, 0.0)
                cross[h] = a if cross[h] is None else cross[h] + a
            bs = half
        for h in heads:
            o[h] = o[h] + jnp.dot(cross[h].astype(BF16), hi_ref[:, sls[h]].astype(BF16),
                                  preferred_element_type=F32)
        for h in heads:
            diag = []
            for j in range(nsub):
                rs = slice(SUB * j, SUB * (j + 1))
                qj = q_sc[rs, sls[h]]
                bj = b_sc[rs, sls[h]]
                oj = jnp.zeros((SUB, HEAD_DIM), F32)
                for s_loc in range(SUB):
                    r = SUB * j + s_loc
                    pr = qj * k_sc[r:r + 1, sls[h]] * jnp.exp2(bj - b_sc[r:r + 1, sls[h]])
                    a = jnp.sum(pr, axis=1, keepdims=True)
                    keep = (row8 <= s_loc) if reverse else (row8 >= s_loc)
                    oj = oj + jnp.where(keep, a, 0.0) * hi_ref[r:r + 1, sls[h]]
                diag.append(oj)
            o_ref[:, sls[h]] = o[h] + jnp.concatenate(diag, axis=0)
        for h in heads:
            b_last = b_sc[last:last + 1, sls[h]]
            kend = (k_sc[:, sls[h]] * jnp.exp2(b_last - b_sc[:, sls[h]])).astype(BF16)
            st_ref[h] = st_ref[h] * jnp.exp2(b_last) + lax.dot_general(
                hi_ref[:, sls[h]].astype(BF16), kend, TN_DIMS, preferred_element_type=F32)


def _hgrn_scan(proj, lbp, *, nh, gq, gf, gi, reverse):
    t = proj.shape[0]
    w = nh * HEAD_DIM
    nc = t // CHUNK
    cidx = (lambda i: nc - 1 - i) if reverse else (lambda i: i)
    spec = lambda g: pl.BlockSpec((CHUNK, w), lambda i: (cidx(i), g))
    return pl.pallas_call(
        functools.partial(_hgrn_body, nh=nh, reverse=reverse),
        grid=(nc,),
        in_specs=[spec(gq), spec(gf), spec(gi),
                  pl.BlockSpec(lbp.shape, lambda i: (0, 0, 0))],
        out_specs=pl.BlockSpec((CHUNK, w), lambda i: (cidx(i), 0)),
        out_shape=jax.ShapeDtypeStruct((t, w), F32),
        scratch_shapes=[pltpu.VMEM((nh, HEAD_DIM, HEAD_DIM), F32)] + [pltpu.VMEM((CHUNK, w), F32)] * 3,
        compiler_params=_params(1),
        name="hgrn2_bw" if reverse else "hgrn2_fw",
    )(proj, proj, proj, lbp)


HALO = 8


def _conv_body(x_ref, xp_ref, xn_ref, w_ref, o_ref, xe_ref, *, nh, tt, kw):
    i = pl.program_id(0)
    g = pl.program_id(1)
    n = pl.num_programs(0)
    xe_ref[0:HALO] = jnp.where(i > 0, xp_ref[...], 0.0)
    xe_ref[HALO:HALO + tt] = x_ref[...]
    xe_ref[HALO + tt:HALO + tt + HALO] = jnp.where(i < n - 1, xn_ref[...], 0.0)
    pad = (kw - 1) // 2
    acc = None
    for j in range(kw):
        term = w_ref[j:j + 1] * xe_ref[HALO - pad + j:HALO - pad + j + tt]
        acc = term if acc is None else acc + term
    y = acc * _sigmoid(acc)
    for h in range(nh):
        sl = slice(h * HEAD_DIM, (h + 1) * HEAD_DIM)
        yh = y[:, sl]
        inv = lax.rsqrt(jnp.sum(yh * yh, axis=1, keepdims=True) + EPS)
        scale = jnp.where(g == 0, inv * (HEAD_DIM ** -0.5), jnp.where(g == 1, inv, 1.0))
        o_ref[0, :, sl] = yh * scale


def _conv_qkv(proj, conv_w, *, nh, g0):
    t = proj.shape[0]
    w = nh * HEAD_DIM
    kw = conv_w.shape[0]
    tt = _pick(t, (256, 128, 64))
    nb = tt // HALO
    last = t // HALO - 1
    return pl.pallas_call(
        functools.partial(_conv_body, nh=nh, tt=tt, kw=kw),
        grid=(t // tt, 3),
        in_specs=[pl.BlockSpec((tt, w), lambda i, g: (i, g0 + g)),
                  pl.BlockSpec((HALO, w), lambda i, g: (jnp.maximum(i * nb - 1, 0), g0 + g)),
                  pl.BlockSpec((HALO, w), lambda i, g: (jnp.minimum((i + 1) * nb, last), g0 + g)),
                  pl.BlockSpec((kw, w), lambda i, g: (0, g))],
        out_specs=pl.BlockSpec((1, tt, w), lambda i, g: (g, i, 0)),
        out_shape=jax.ShapeDtypeStruct((3, t, w), F32),
        scratch_shapes=[pltpu.VMEM((tt + 2 * HALO, w), F32)],
        compiler_params=_params(2),
        name="gdn_conv",
    )(proj, proj, proj, conv_w)


def _gdn_bidir_body(qf_ref, kf_ref, vf_ref, gabf_ref, qb_ref, kb_ref, vb_ref, gabb_ref, arow_ref, dtrow_ref,
                    of_ref, ob_ref, st_ref, *, nh):
    @pl.when(pl.program_id(0) == 0)
    def _():
        st_ref[...] = jnp.zeros_like(st_ref)

    assert nh % 2 == 0 and 2 * CHUNK == LANES
    r = lax.broadcasted_iota(jnp.int32, (CHUNK, LANES), 0)
    lane = lax.broadcasted_iota(jnp.int32, (CHUNK, LANES), 1)
    c = lane & (CHUNK - 1)
    lo = lane < CHUNK
    eye = (r == c).astype(F32)

    def diag2(x, y):
        z = jnp.zeros_like(x)
        return jnp.concatenate([jnp.concatenate([x, z], axis=1), jnp.concatenate([z, y], axis=1)], axis=0)

    def diag_packed(p):
        z = jnp.zeros_like(p)
        return jnp.concatenate([jnp.where(lo, p, z), jnp.where(lo, z, p)], axis=0)

    refs = ((qf_ref, kf_ref, vf_ref, of_ref), (qb_ref, kb_ref, vb_ref, ob_ref))
    gam_all, gam_t, beta_all, incl, strict = [], [], [], [], []
    for s, gab_ref in enumerate((gabf_ref, gabb_ref)):
        gab = gab_ref[...]
        xs = gab + dtrow_ref[...]
        softplus = jnp.maximum(xs, 0.0) + jnp.log(1.0 + jnp.exp(-jnp.abs(xs)))
        dec = -jnp.exp(arow_ref[...]) * softplus
        beta_all.append(_sigmoid(gab))
        ga = jnp.dot(_tri(bool(s)).astype(F32), dec, precision=HIGHEST, preferred_element_type=F32)
        gam_all.append(ga)
        gam_t.append(ga.T)
        incl.append((c >= r) if s else (c <= r))
        strict.append((c > r) if s else (c < r))

    sls = [slice(h * HEAD_DIM, (h + 1) * HEAD_DIM) for h in range(nh)]
    heads = [(s, h) for h in range(nh) for s in (0, 1)]
    pairs = [(s, j) for j in range(nh // 2) for s in (0, 1)]
    col = lambda s, h: s * nh + h
    gam_c = {(s, h): gam_all[s][:, col(s, h):col(s, h) + 1] for s, h in heads}
    beta_c = {(s, h): beta_all[s][:, 2 * nh + col(s, h):2 * nh + col(s, h) + 1] for s, h in heads}
    gam_cp = {(s, j): jnp.where(lo, gam_c[s, 2 * j], gam_c[s, 2 * j + 1]) for s, j in pairs}
    beta_cp = {(s, j): jnp.where(lo, beta_c[s, 2 * j], beta_c[s, 2 * j + 1]) for s, j in pairs}
    gam_rp = {(s, j): jnp.concatenate([gam_t[s][col(s, 2 * j):col(s, 2 * j) + 1, :],
                                       gam_t[s][col(s, 2 * j + 1):col(s, 2 * j + 1) + 1, :]], axis=1)
              for s, j in pairs}
    rel = {p: jnp.where(incl[p[0]], jnp.exp(gam_cp[p] - gam_rp[p]), 0.0) for p in pairs}
    kb = {(s, h): refs[s][1][0, :, sls[h]].astype(BF16) for s, h in heads}
    kd = {(s, j): diag2(kb[s, 2 * j], kb[s, 2 * j + 1]) for s, j in pairs}
    kk = {(s, j): lax.dot_general(jnp.concatenate([kb[s, 2 * j], kb[s, 2 * j + 1]], axis=1), kd[s, j], NT_DIMS,
                                  preferred_element_type=F32) for s, j in pairs}
    qk = {(s, j): lax.dot_general(
        jnp.concatenate([refs[s][0][0, :, sls[2 * j]], refs[s][0][0, :, sls[2 * j + 1]]], axis=1).astype(BF16),
        kd[s, j], NT_DIMS, preferred_element_type=F32) for s, j in pairs}
    qk = {p: jnp.where(incl[p[0]], qk[p] * rel[p], 0.0).astype(BF16) for p in pairs}
    m = {p: jnp.where(strict[p[0]], -(beta_cp[p] * kk[p] * rel[p]), 0.0) for p in pairs}
    tinv = {p: eye + m[p] for p in pairs}
    pw = {p: m[p].astype(BF16) for p in pairs}
    sq = 1
    while 2 * sq < CHUNK:
        pw = {p: jnp.dot(pw[p], diag_packed(pw[p]), preferred_element_type=F32).astype(BF16) for p in pairs}
        tinv = {p: tinv[p] + jnp.dot(tinv[p].astype(BF16), diag_packed(pw[p]), preferred_element_type=F32)
                for p in pairs}
        sq *= 2
    rhs = {}
    for s, h in heads:
        k = refs[s][1][0, :, sls[h]]
        rhs[s, h] = jnp.concatenate([refs[s][2][0, :, sls[h]] * beta_c[s, h],
                                     k * (beta_c[s, h] * jnp.exp(gam_c[s, h]))], axis=1).astype(BF16)
    solp = {(s, j): jnp.dot(tinv[s, j].astype(BF16), diag2(rhs[s, 2 * j], rhs[s, 2 * j + 1]),
                            preferred_element_type=F32) for s, j in pairs}
    sol = {(s, h): solp[s, h // 2][:, (h % 2) * 2 * HEAD_DIM:(h % 2 + 1) * 2 * HEAD_DIM] for s, h in heads}
    stb = {(s, h): st_ref[s, h].astype(BF16) for s, h in heads}
    vnb = {x: (sol[x][:, :HEAD_DIM]
               - lax.dot_general(sol[x][:, HEAD_DIM:].astype(BF16), stb[x], NT_DIMS, preferred_element_type=F32)
               ).astype(BF16) for x in heads}
    intra = {(s, j): jnp.dot(qk[s, j], diag2(vnb[s, 2 * j], vnb[s, 2 * j + 1]), preferred_element_type=F32)
             for s, j in pairs}
    for s, h in heads:
        qdec = (refs[s][0][0, :, sls[h]] * jnp.exp(gam_c[s, h])).astype(BF16)
        refs[s][3][:, sls[h]] = (lax.dot_general(qdec, stb[s, h], NT_DIMS, preferred_element_type=F32)
                                 + intra[s, h // 2][:, (h % 2) * HEAD_DIM:(h % 2 + 1) * HEAD_DIM])
    for s, h in heads:
        last = 0 if s else CHUNK - 1
        gam_last = gam_c[s, h][last:last + 1]
        kdec = (refs[s][1][0, :, sls[h]] * jnp.exp(gam_last - gam_c[s, h])).astype(BF16)
        st_ref[s, h] = st_ref[s, h] * jnp.exp(gam_last) + lax.dot_general(vnb[s, h], kdec, TN_DIMS,
                                                                          preferred_element_type=F32)


def _gdn_scan_bidir(qkv, gab, arow, dtrow, *, nh):
    t = qkv.shape[1]
    w = nh * HEAD_DIM
    nc = t // CHUNK
    fw = lambda i: i
    bw = lambda i: nc - 1 - i
    spec = lambda g, cidx: pl.BlockSpec((1, CHUNK, w), lambda i: (g, cidx(i), 0))
    gspec = lambda cidx: pl.BlockSpec((CHUNK, LANES), lambda i: (cidx(i), 0))
    row = pl.BlockSpec((1, LANES), lambda i: (0, 0))
    out_shape = jax.ShapeDtypeStruct((t, w), F32)
    return pl.pallas_call(
        functools.partial(_gdn_bidir_body, nh=nh),
        grid=(nc,),
        in_specs=[spec(0, fw), spec(1, fw), spec(2, fw), gspec(fw),
                  spec(0, bw), spec(1, bw), spec(2, bw), gspec(bw), row, row],
        out_specs=[pl.BlockSpec((CHUNK, w), lambda i: (fw(i), 0)), pl.BlockSpec((CHUNK, w), lambda i: (bw(i), 0))],
        out_shape=[out_shape, out_shape],
        scratch_shapes=[pltpu.VMEM((2, nh, HEAD_DIM, HEAD_DIM), F32)],
        compiler_params=_params(1),
        name="gdn_bidir",
    )(qkv, qkv, qkv, gab, qkv, qkv, qkv, gab, arow, dtrow)


def _gnorm_body(af_ref, ab_ref, bf_ref, bb_ref, hz_ref, gz_ref, wa_ref, wb_ref, o_ref, *, nh):
    w = nh * HEAD_DIM
    for grp, (f_ref, b_ref, z_ref, w_ref) in enumerate(((af_ref, ab_ref, hz_ref, wa_ref),
                                                        (bf_ref, bb_ref, gz_ref, wb_ref))):
        for h in range(nh):
            sl = slice(h * HEAD_DIM, (h + 1) * HEAD_DIM)
            o = f_ref[:, sl] + b_ref[:, sl]
            y = o * lax.rsqrt(jnp.mean(o * o, axis=1, keepdims=True) + EPS) * w_ref[...]
            z = z_ref[:, sl]
            osl = slice(grp * w + h * HEAD_DIM, grp * w + (h + 1) * HEAD_DIM)
            o_ref[:, osl] = (y * (z * _sigmoid(z))).astype(o_ref.dtype)


def _gated_norm(oa_f, oa_b, ob_f, ob_b, proj, gz, wa, wb, *, nh, ghz):
    t = oa_f.shape[0]
    w = nh * HEAD_DIM
    tm = _pick(t, (256, 128, 64))
    spec = pl.BlockSpec((tm, w), lambda i: (i, 0))
    return pl.pallas_call(
        functools.partial(_gnorm_body, nh=nh),
        grid=(t // tm,),
        in_specs=[spec, spec, spec, spec,
                  pl.BlockSpec((tm, w), lambda i: (i, ghz)),
                  spec,
                  pl.BlockSpec((1, HEAD_DIM), lambda i: (0, 0)),
                  pl.BlockSpec((1, HEAD_DIM), lambda i: (0, 0))],
        out_specs=pl.BlockSpec((tm, 2 * w), lambda i: (i, 0)),
        out_shape=jax.ShapeDtypeStruct((t, 2 * w), BF16),
        compiler_params=_params(1),
        name="gated_head_norm",
    )(oa_f, oa_b, ob_f, ob_b, proj, gz, wa.reshape(1, HEAD_DIM), wb.reshape(1, HEAD_DIM))


def _cand_rows():
    out = []
    for a in range(PEER_TOPK):
        out.append((a, min(PEER_TOPK, PEER_TOPK // (a + 1))))
    return out


N_CAND = sum(nb for _, nb in _cand_rows())
N_CAND_PAD = -(-N_CAND // 8) * 8


def _route_body(qry_ref, sk_ref, c1_ref, e1_ref, r2_ref, e2_ref, top_ref, cand_ref, *, nheads):
    ninf = -jnp.inf
    for h in range(nheads):
        scores, ranks = [], []
        for p in range(2):
            j = 2 * h + p
            qhp = qry_ref[:, j * PEER_NKEYS:(j + 1) * PEER_NKEYS].astype(BF16)
            keys = sk_ref[h, p].astype(BF16)
            s = lax.dot_general(keys, qhp, NT_DIMS, preferred_element_type=F32)
            scores.append(s)
            wv = s
            rk = jnp.full(s.shape, float(PEER_NKEYS), F32)
            for rnk in range(PEER_TOPK):
                mx = jnp.max(wv, axis=0, keepdims=True)
                top_ref[p * PEER_TOPK + rnk:p * PEER_TOPK + rnk + 1, :] = mx
                sel = wv == mx
                if p == 1:
                    rk = jnp.where(sel, float(rnk), rk)
                wv = jnp.where(sel, ninf, wv)
            ranks.append(rk)
        t1 = top_ref[0:PEER_TOPK]
        t2 = top_ref[PEER_TOPK:2 * PEER_TOPK]
        cand_ref[...] = jnp.full(cand_ref.shape, ninf, F32)
        off = 0
        for a, nb in _cand_rows():
            cand_ref[off:off + nb] = t1[a:a + 1] + t2[0:nb]
            off += nb
        cnd = cand_ref[...]
        cmax = t1[0:1] + t2[0:1]
        z = jnp.zeros_like(cmax)
        mx = cmax
        for rnk in range(PEER_TOPK):
            mx = jnp.max(cnd, axis=0, keepdims=True)
            z = z + jnp.exp(mx - cmax)
            cnd = jnp.where(cnd == mx, ninf, cnd)
        s1, s2 = scores
        c1 = jnp.zeros_like(s1)
        for a in range(PEER_TOPK):
            cnt = jnp.sum(jnp.where((t1[a:a + 1] + t2) >= mx, 1.0, 0.0), axis=0, keepdims=True)
            c1 = jnp.where(s1 == t1[a:a + 1], cnt, c1)
        c1_ref[h] = c1
        e1_ref[h] = jnp.exp(s1 - t1[0:1]) / z
        r2_ref[h] = ranks[1].astype(r2_ref.dtype)
        e2_ref[h] = jnp.exp(s2 - t2[0:1]).astype(e2_ref.dtype)


def _peer_route(qry, sub_keys):
    t = qry.shape[0]
    nheads = sub_keys.shape[0]
    tt = _pick(t, (512, 256, 128))
    big = pl.BlockSpec((nheads, PEER_NKEYS, tt), lambda i: (0, 0, i))
    f32_shape = jax.ShapeDtypeStruct((nheads, PEER_NKEYS, t), F32)
    b16_shape = jax.ShapeDtypeStruct((nheads, PEER_NKEYS, t), BF16)
    return pl.pallas_call(
        functools.partial(_route_body, nheads=nheads),
        grid=(t // tt,),
        in_specs=[pl.BlockSpec((tt, qry.shape[1]), lambda i: (i, 0)),
                  pl.BlockSpec(sub_keys.shape, lambda i: (0, 0, 0, 0))],
        out_specs=[big, big, big, big],
        out_shape=[f32_shape, f32_shape, b16_shape, b16_shape],
        scratch_shapes=[pltpu.VMEM((2 * PEER_TOPK, tt), F32), pltpu.VMEM((N_CAND_PAD, tt), F32)],
        compiler_params=_params(1),
        name="peer_route",
    )(qry, sub_keys)


def _peer_body(xn_ref, u_ref, v_ref, c1_ref, e1_ref, r2_ref, e2_ref, o_ref, g_ref, *, nheads, te):
    @pl.when(pl.program_id(1) == 0)
    def _():
        o_ref[...] = jnp.zeros_like(o_ref)

    nrow = te // PEER_NKEYS
    base = (pl.program_id(1) * nrow) % ROW_TILE
    hid = lax.dot_general(u_ref[...], xn_ref[...], NT_DIMS, preferred_element_type=F32)
    zero = jnp.zeros((), g_ref.dtype)
    for ii in range(nrow):
        acc = None
        for h in range(nheads):
            c1row = c1_ref[h, pl.ds(base + ii, 1), :].astype(g_ref.dtype)
            e1row = e1_ref[h, pl.ds(base + ii, 1), :].astype(g_ref.dtype)
            term = jnp.where(r2_ref[h] < c1row, e2_ref[h] * e1row, zero)
            acc = term if acc is None else acc + term
        g_ref[ii * PEER_NKEYS:(ii + 1) * PEER_NKEYS, :] = acc
    act = (0.5 * hid * (1.0 + lax.erf(hid * INV_SQRT2)) * g_ref[...].astype(F32)).astype(BF16)
    o_ref[...] += lax.dot_general(act, v_ref[...], TN_DIMS, preferred_element_type=F32)


def _peer_dense(xn, u, v, c1, e1, r2, e2):
    t, d = xn.shape
    ne = u.shape[0]
    nheads = c1.shape[0]
    tt = _pick(t, (512, 256, 128))
    te = _pick(ne, (512, 256, 128))
    nrow = te // PEER_NKEYS
    big = pl.BlockSpec((nheads, PEER_NKEYS, tt), lambda i, e: (0, 0, i))
    assert ROW_TILE % nrow == 0
    rows = pl.BlockSpec((nheads, ROW_TILE, tt), lambda i, e: (0, (e * nrow) // ROW_TILE, i))
    return pl.pallas_call(
        functools.partial(_peer_body, nheads=nheads, te=te),
        grid=(t // tt, ne // te),
        in_specs=[pl.BlockSpec((tt, d), lambda i, e: (i, 0)),
                  pl.BlockSpec((te, d), lambda i, e: (e, 0)),
                  pl.BlockSpec((te, d), lambda i, e: (e, 0)),
                  rows, rows, big, big],
        out_specs=pl.BlockSpec((tt, d), lambda i, e: (i, 0)),
        out_shape=jax.ShapeDtypeStruct((t, d), F32),
        scratch_shapes=[pltpu.VMEM((te, tt), BF16)],
        compiler_params=_params(2),
        name="peer_dense",
    )(xn, u, v, c1, e1, r2, e2)


def _add_rms_body(a_ref, b_ref, w_ref, s_ref, n_ref):
    x = a_ref[...] + b_ref[...]
    s_ref[...] = x
    ms = jnp.mean(x * x, axis=-1, keepdims=True)
    n_ref[...] = (x * lax.rsqrt(ms + EPS) * w_ref[...]).astype(n_ref.dtype)


def _add_rmsnorm(a, b, w):
    t, d = a.shape
    tm = _pick(t, (256, 128, 64, 8))
    spec = pl.BlockSpec((tm, d), lambda i: (i, 0))
    return pl.pallas_call(
        _add_rms_body,
        grid=(t // tm,),
        in_specs=[spec, spec, pl.BlockSpec((1, d), lambda i: (0, 0))],
        out_specs=[spec, spec],
        out_shape=[jax.ShapeDtypeStruct((t, d), F32), jax.ShapeDtypeStruct((t, d), BF16)],
        compiler_params=_params(1),
        name="add_rmsnorm",
    )(a, b, w.reshape(1, d))


def _ple_body(xn_ref, wg_ref, p_ref, wp_ref, h_ref, pw_ref, fw_ref, o_ref, emb_ref, *, tn):
    j = pl.program_id(1)

    @pl.when(j == 0)
    def _():
        raw = jnp.dot(p_ref[...], wp_ref[...], preferred_element_type=F32)
        ms = jnp.mean(raw * raw, axis=-1, keepdims=True)
        emb_ref[...] = raw * lax.rsqrt(ms + EPS) * pw_ref[...]

    cs = pl.ds(pl.multiple_of(j * tn, tn), tn)
    gate = _sigmoid(jnp.dot(xn_ref[...], wg_ref[...], preferred_element_type=F32))
    o_ref[:, cs] = h_ref[...] + gate * emb_ref[:, cs]

    @pl.when(j == pl.num_programs(1) - 1)
    def _():
        x = o_ref[...]
        ms = jnp.mean(x * x, axis=-1, keepdims=True)
        o_ref[...] = x * lax.rsqrt(ms + EPS) * fw_ref[...]


def _ple_final(xn, w_gate, p, w_proj, h, post_w, final_w):
    t, d = h.shape
    pd = p.shape[1]
    tm = _pick(t, (512, 256, 128, 64, 8))
    tn = _pick(d, (512, 256, 128))
    return pl.pallas_call(
        functools.partial(_ple_body, tn=tn),
        grid=(t // tm, d // tn),
        in_specs=[pl.BlockSpec((tm, d), lambda i, j: (i, 0)),
                  pl.BlockSpec((d, tn), lambda i, j: (0, j)),
                  pl.BlockSpec((tm, pd), lambda i, j: (i, 0)),
                  pl.BlockSpec((pd, d), lambda i, j: (0, 0)),
                  pl.BlockSpec((tm, tn), lambda i, j: (i, j)),
                  pl.BlockSpec((1, d), lambda i, j: (0, 0)),
                  pl.BlockSpec((1, d), lambda i, j: (0, 0))],
        out_specs=pl.BlockSpec((tm, d), lambda i, j: (i, 0)),
        out_shape=jax.ShapeDtypeStruct((t, d), F32),
        scratch_shapes=[pltpu.VMEM((tm, d), F32)],
        compiler_params=_params(2),
        name="ple_final",
    )(xn, w_gate, p, w_proj, h, post_w.reshape(1, d), final_w.reshape(1, d))


def kernel(x, p, attn_norm_w, w_in, hg_lower_bound, gd_conv_w, gd_A_log, gd_dt_bias, hg_out_norm_w, gd_out_norm_w, w_out, ffn_norm_w, peer_w_query, peer_sub_keys, peer_u, peer_v, ple_norm_w, ple_w_gate, ple_w_proj, ple_post_norm_w, final_norm_w):
    bsz, seq, d = x.shape
    depth = w_in.shape[0]
    assert bsz == 1 and depth == 1
    w = d // 2
    nh = w // HEAD_DIM
    assert 4 * nh <= LANES
    t = bsz * seq
    h = x.reshape(t, d)
    i = 0

    n_main = 8 * w
    wi = _to_bf16(w_in, i)
    xn = _rmsnorm(h, attn_norm_w[i])
    proj = _matmul(xn, wi, name="in_proj", ncols=n_main)
    gab = _matmul(xn, wi, name="in_proj_gates", col0=n_main, ncols=LANES)
    gz = _matmul(xn, w_in[i][:, n_main + 4 * nh:].astype(BF16), name="in_proj_gz")

    oa = []
    for dr, rev in ((0, False), (1, True)):
        lbp = hg_lower_bound[:, dr, :].reshape(depth + 1, 1, w)
        oa.append(_hgrn_scan(proj, lbp, nh=nh, gq=0, gf=1 + dr, gi=3, reverse=rev))

    qkv = _conv_qkv(proj, gd_conv_w[i], nh=nh, g0=5)
    pad = jnp.zeros((LANES - 2 * nh,), F32)
    arow = jnp.concatenate([gd_A_log[i].reshape(-1), pad]).reshape(1, LANES)
    dtrow = jnp.concatenate([gd_dt_bias[i].reshape(-1), pad]).reshape(1, LANES)
    ob = _gdn_scan_bidir(qkv, gab, arow, dtrow, nh=nh)

    mix_in = _gated_norm(oa[0], oa[1], ob[0], ob[1], proj, gz, hg_out_norm_w[i], gd_out_norm_w[i],
                         nh=nh, ghz=4)
    h1 = _matmul(mix_in, w_out[i].astype(BF16), residual=h, name="out_proj")

    xn2 = _rmsnorm(h1, ffn_norm_w[i])
    qry = _matmul(xn2, peer_w_query[i].astype(BF16), name="peer_query")
    c1, e1, r2, e2 = _peer_route(qry, peer_sub_keys[i])
    peer_out = _peer_dense(xn2, _to_bf16(peer_u, i), _to_bf16(peer_v, i), c1, e1, r2, e2)
    h2, xn3 = _add_rmsnorm(h1, peer_out, ple_norm_w[i])

    out = _ple_final(xn3, ple_w_gate[i].astype(BF16), p[i].reshape(t, -1).astype(BF16),
                     ple_w_proj[i].astype(BF16), h2, ple_post_norm_w[i], final_norm_w)
    return out.reshape(bsz, seq, d)
```

```python
import functools

import jax
import jax.numpy as jnp
from jax import lax
from jax.experimental import pallas as pl
from jax.experimental.pallas import tpu as pltpu

F32 = jnp.float32
BF16 = jnp.bfloat16
EPS = 1e-6
HEAD_DIM = 128
CHUNK = 64
SUB = 8
HGRN_GROUP = 8
PEER_TOPK = 16
PEER_NKEYS = 128
LANES = 128
ROW_TILE = 8
VMEM_LIMIT_BYTES = 56 * 1024 * 1024
CAST_BLOCK_BYTES = 8 * 1024 * 1024
HIGHEST = lax.Precision.HIGHEST
NT_DIMS = (((1,), (1,)), ((), ()))
TN_DIMS = (((0,), (0,)), ((), ()))
INV_SQRT2 = 0.7071067811865476


def _params(n_axes, **kw):
    return pltpu.CompilerParams(dimension_semantics=("arbitrary",) * n_axes,
                                vmem_limit_bytes=VMEM_LIMIT_BYTES, **kw)


def _sigmoid(x):
    return 1.0 / (1.0 + jnp.exp(-x))


def _pick(n, prefs):
    for t in prefs:
        if n % t == 0:
            return t
    return n


def _rms_body(x_ref, w_ref, o_ref):
    x = x_ref[...]
    ms = jnp.mean(x * x, axis=-1, keepdims=True)
    o_ref[...] = (x * lax.rsqrt(ms + EPS) * w_ref[...]).astype(o_ref.dtype)


def _rmsnorm(x, w):
    t, d = x.shape
    tm = _pick(t, (512, 256, 128, 64, 8))
    return pl.pallas_call(
        _rms_body,
        grid=(t // tm,),
        in_specs=[pl.BlockSpec((tm, d), lambda i: (i, 0)),
                  pl.BlockSpec((1, d), lambda i: (0, 0))],
        out_specs=pl.BlockSpec((tm, d), lambda i: (i, 0)),
        out_shape=jax.ShapeDtypeStruct((t, d), BF16),
        compiler_params=_params(1),
        name="rmsnorm",
    )(x, w.reshape(1, d))


def _cast_body(x_ref, o_ref):
    o_ref[...] = x_ref[0].astype(o_ref.dtype)


def _to_bf16(w3, layer):
    _, m, n = w3.shape
    tm = _pick(m, tuple(c for c in (512, 256, 128, 64, 32, 16, 8) if c * n * 4 <= CAST_BLOCK_BYTES))
    return pl.pallas_call(
        _cast_body,
        grid=(m // tm,),
        in_specs=[pl.BlockSpec((1, tm, n), lambda i: (layer, i, 0))],
        out_specs=pl.BlockSpec((tm, n), lambda i: (i, 0)),
        out_shape=jax.ShapeDtypeStruct((m, n), BF16),
        compiler_params=_params(1),
        name="to_bf16",
    )(w3)


def _mm_body(a_ref, b_ref, o_ref):
    o_ref[...] = jnp.dot(a_ref[...], b_ref[...], preferred_element_type=F32)


def _mm_res_body(a_ref, b_ref, r_ref, o_ref):
    o_ref[...] = r_ref[...] + jnp.dot(a_ref[...], b_ref[...], preferred_element_type=F32)


def _matmul(a, b, residual=None, name="matmul", col0=0, ncols=None):
    m, k = a.shape
    n = b.shape[1] if ncols is None else ncols
    tm = _pick(m, (1024, 512, 256, 128, 64, 8))
    tn = _pick(n, (512, 256, 128))
    assert col0 % tn == 0
    jb = col0 // tn
    in_specs = [pl.BlockSpec((tm, k), lambda i, j: (i, 0)),
                pl.BlockSpec((k, tn), lambda i, j: (0, jb + j))]
    args = [a, b]
    body = _mm_body
    if residual is not None:
        in_specs.append(pl.BlockSpec((tm, tn), lambda i, j: (i, j)))
        args.append(residual)
        body = _mm_res_body
    return pl.pallas_call(
        body,
        grid=(m // tm, n // tn),
        in_specs=in_specs,
        out_specs=pl.BlockSpec((tm, tn), lambda i, j: (i, j)),
        out_shape=jax.ShapeDtypeStruct((m, n), F32),
        compiler_params=_params(2),
        name=name,
    )(*args)


def _tri(reverse, strict=False):
    r = lax.broadcasted_iota(jnp.int32, (CHUNK, CHUNK), 0)
    c = lax.broadcasted_iota(jnp.int32, (CHUNK, CHUNK), 1)
    if reverse:
        return (c > r) if strict else (c >= r)
    return (c < r) if strict else (c <= r)


def _hgrn_body(hq_ref, hf_ref, hi_ref, lbp_ref, o_ref, st_ref, q_sc, k_sc, b_sc, *, nh, reverse):
    @pl.when(pl.program_id(0) == 0)
    def _():
        st_ref[...] = jnp.zeros_like(st_ref)

    nl = lbp_ref.shape[0]
    mx = lbp_ref[0]
    for l in range(1, nl):
        mx = jnp.maximum(mx, lbp_ref[l])
    den = jnp.exp(lbp_ref[0] - mx)
    e0 = den
    for l in range(1, nl):
        den = den + jnp.exp(lbp_ref[l] - mx)
    lb = e0 / den

    zq = hq_ref[...]
    s = _sigmoid(hf_ref[...])
    q_sc[...] = zq * _sigmoid(zq) * (HEAD_DIM ** -0.5)
    k_sc[...] = (1.0 - lb) * (1.0 - s)
    logf = jnp.log2(lb + (1.0 - lb) * s)
    b_sc[...] = jnp.dot(_tri(reverse).astype(F32), logf, precision=HIGHEST, preferred_element_type=F32)

    row_c = lax.broadcasted_iota(jnp.int32, (CHUNK, CHUNK), 0)
    col_c = lax.broadcasted_iota(jnp.int32, (CHUNK, CHUNK), 1)
    row8 = lax.broadcasted_iota(jnp.int32, (SUB, 1), 0)
    nsub = CHUNK // SUB
    last = 0 if reverse else CHUNK - 1

    for g0 in range(0, nh, HGRN_GROUP):
        heads = list(range(g0, min(g0 + HGRN_GROUP, nh)))
        sls = {h: slice(h * HEAD_DIM, (h + 1) * HEAD_DIM) for h in heads}
        stb = {h: st_ref[h].astype(BF16) for h in heads}
        o = {h: lax.dot_general((q_sc[:, sls[h]] * jnp.exp2(b_sc[:, sls[h]])).astype(BF16), stb[h], NT_DIMS,
                                preferred_element_type=F32) for h in heads}
        cross = {h: None for h in heads}
        bs = CHUNK
        while bs >= 2 * SUB:
            half = bs // 2
            same = (row_c // bs) == (col_c // bs)
            for h in heads:
                qparts, kparts = [], []
                for s0 in range(0, CHUNK, bs):
                    lo_rows, hi_rows = slice(s0, s0 + half), slice(s0 + half, s0 + bs)
                    qs, ks, rr = (lo_rows, hi_rows, s0 + half) if reverse else (hi_rows, lo_rows, s0 + half - 1)
                    ref = b_sc[rr:rr + 1, sls[h]]
                    qpart = q_sc[qs, sls[h]] * jnp.exp2(b_sc[qs, sls[h]] - ref)
                    kpart = k_sc[ks, sls[h]] * jnp.exp2(ref - b_sc[ks, sls[h]])
                    zero = jnp.zeros_like(qpart)
                    qparts += [qpart, zero] if reverse else [zero, qpart]
                    kparts += [zero, kpart] if reverse else [kpart, zero]
                a = lax.dot_general(jnp.concatenate(qparts, axis=0).astype(BF16),
                                    jnp.concatenate(kparts, axis=0).astype(BF16), NT_DIMS,
                                    preferred_element_type=F32)
                if bs < CHUNK:
                    a = jnp.where(same, a, 0.0)
                cross[h] = a if cross[h] is None else cross[h] + a
            bs = half
        for h in heads:
            o[h] = o[h] + jnp.dot(cross[h].astype(BF16), hi_ref[:, sls[h]].astype(BF16),
                                  preferred_element_type=F32)
        for h in heads:
            diag = []
            for j in range(nsub):
                rs = slice(SUB * j, SUB * (j + 1))
                qj = q_sc[rs, sls[h]]
                bj = b_sc[rs, sls[h]]
                oj = jnp.zeros((SUB, HEAD_DIM), F32)
                for s_loc in range(SUB):
                    r = SUB * j + s_loc
                    pr = qj * k_sc[r:r + 1, sls[h]] * jnp.exp2(bj - b_sc[r:r + 1, sls[h]])
                    a = jnp.sum(pr, axis=1, keepdims=True)
                    keep = (row8 <= s_loc) if reverse else (row8 >= s_loc)
                    oj = oj + jnp.where(keep, a, 0.0) * hi_ref[r:r + 1, sls[h]]
                diag.append(oj)
            o_ref[:, sls[h]] = o[h] + jnp.concatenate(diag, axis=0)
        for h in heads:
            b_last = b_sc[last:last + 1, sls[h]]
            kend = (k_sc[:, sls[h]] * jnp.exp2(b_last - b_sc[:, sls[h]])).astype(BF16)
            st_ref[h] = st_ref[h] * jnp.exp2(b_last) + lax.dot_general(
                hi_ref[:, sls[h]].astype(BF16), kend, TN_DIMS, preferred_element_type=F32)


def _hgrn_scan(proj, lbp, *, nh, gq, gf, gi, reverse):
    t = proj.shape[0]
    w = nh * HEAD_DIM
    nc = t // CHUNK
    cidx = (lambda i: nc - 1 - i) if reverse else (lambda i: i)
    spec = lambda g: pl.BlockSpec((CHUNK, w), lambda i: (cidx(i), g))
    return pl.pallas_call(
        functools.partial(_hgrn_body, nh=nh, reverse=reverse),
        grid=(nc,),
        in_specs=[spec(gq), spec(gf), spec(gi),
                  pl.BlockSpec(lbp.shape, lambda i: (0, 0, 0))],
        out_specs=pl.BlockSpec((CHUNK, w), lambda i: (cidx(i), 0)),
        out_shape=jax.ShapeDtypeStruct((t, w), F32),
        scratch_shapes=[pltpu.VMEM((nh, HEAD_DIM, HEAD_DIM), F32)] + [pltpu.VMEM((CHUNK, w), F32)] * 3,
        compiler_params=_params(1),
        name="hgrn2_bw" if reverse else "hgrn2_fw",
    )(proj, proj, proj, lbp)


HALO = 8


def _conv_body(x_ref, xp_ref, xn_ref, w_ref, o_ref, xe_ref, *, nh, tt, kw):
    i = pl.program_id(0)
    g = pl.program_id(1)
    n = pl.num_programs(0)
    xe_ref[0:HALO] = jnp.where(i > 0, xp_ref[...], 0.0)
    xe_ref[HALO:HALO + tt] = x_ref[...]
    xe_ref[HALO + tt:HALO + tt + HALO] = jnp.where(i < n - 1, xn_ref[...], 0.0)
    pad = (kw - 1) // 2
    acc = None
    for j in range(kw):
        term = w_ref[j:j + 1] * xe_ref[HALO - pad + j:HALO - pad + j + tt]
        acc = term if acc is None else acc + term
    y = acc * _sigmoid(acc)
    for h in range(nh):
        sl = slice(h * HEAD_DIM, (h + 1) * HEAD_DIM)
        yh = y[:, sl]
        inv = lax.rsqrt(jnp.sum(yh * yh, axis=1, keepdims=True) + EPS)
        scale = jnp.where(g == 0, inv * (HEAD_DIM ** -0.5), jnp.where(g == 1, inv, 1.0))
        o_ref[0, :, sl] = yh * scale


def _conv_qkv(proj, conv_w, *, nh, g0):
    t = proj.shape[0]
    w = nh * HEAD_DIM
    kw = conv_w.shape[0]
    tt = _pick(t, (256, 128, 64))
    nb = tt // HALO
    last = t // HALO - 1
    return pl.pallas_call(
        functools.partial(_conv_body, nh=nh, tt=tt, kw=kw),
        grid=(t // tt, 3),
        in_specs=[pl.BlockSpec((tt, w), lambda i, g: (i, g0 + g)),
                  pl.BlockSpec((HALO, w), lambda i, g: (jnp.maximum(i * nb - 1, 0), g0 + g)),
                  pl.BlockSpec((HALO, w), lambda i, g: (jnp.minimum((i + 1) * nb, last), g0 + g)),
                  pl.BlockSpec((kw, w), lambda i, g: (0, g))],
        out_specs=pl.BlockSpec((1, tt, w), lambda i, g: (g, i, 0)),
        out_shape=jax.ShapeDtypeStruct((3, t, w), F32),
        scratch_shapes=[pltpu.VMEM((tt + 2 * HALO, w), F32)],
        compiler_params=_params(2),
        name="gdn_conv",
    )(proj, proj, proj, conv_w)


def _gdn_bidir_body(qf_ref, kf_ref, vf_ref, gabf_ref, qb_ref, kb_ref, vb_ref, gabb_ref, arow_ref, dtrow_ref,
                    of_ref, ob_ref, st_ref, *, nh):
    @pl.when(pl.program_id(0) == 0)
    def _():
        st_ref[...] = jnp.zeros_like(st_ref)

    assert nh % 2 == 0 and 2 * CHUNK == LANES
    r = lax.broadcasted_iota(jnp.int32, (CHUNK, LANES), 0)
    lane = lax.broadcasted_iota(jnp.int32, (CHUNK, LANES), 1)
    c = lane & (CHUNK - 1)
    lo = lane < CHUNK
    eye = (r == c).astype(F32)

    def diag2(x, y):
        z = jnp.zeros_like(x)
        return jnp.concatenate([jnp.concatenate([x, z], axis=1), jnp.concatenate([z, y], axis=1)], axis=0)

    def diag_packed(p):
        z = jnp.zeros_like(p)
        return jnp.concatenate([jnp.where(lo, p, z), jnp.where(lo, z, p)], axis=0)

    refs = ((qf_ref, kf_ref, vf_ref, of_ref), (qb_ref, kb_ref, vb_ref, ob_ref))
    gam_all, gam_t, beta_all, incl, strict = [], [], [], [], []
    for s, gab_ref in enumerate((gabf_ref, gabb_ref)):
        gab = gab_ref[...]
        xs = gab + dtrow_ref[...]
        softplus = jnp.maximum(xs, 0.0) + jnp.log(1.0 + jnp.exp(-jnp.abs(xs)))
        dec = -jnp.exp(arow_ref[...]) * softplus
        beta_all.append(_sigmoid(gab))
        ga = jnp.dot(_tri(bool(s)).astype(F32), dec, precision=HIGHEST, preferred_element_type=F32)
        gam_all.append(ga)
        gam_t.append(ga.T)
        incl.append((c >= r) if s else (c <= r))
        strict.append((c > r) if s else (c < r))

    sls = [slice(h * HEAD_DIM, (h + 1) * HEAD_DIM) for h in range(nh)]
    heads = [(s, h) for h in range(nh) for s in (0, 1)]
    pairs = [(s, j) for j in range(nh // 2) for s in (0, 1)]
    col = lambda s, h: s * nh + h
    gam_c = {(s, h): gam_all[s][:, col(s, h):col(s, h) + 1] for s, h in heads}
    beta_c = {(s, h): beta_all[s][:, 2 * nh + col(s, h):2 * nh + col(s, h) + 1] for s, h in heads}
    gam_cp = {(s, j): jnp.where(lo, gam_c[s, 2 * j], gam_c[s, 2 * j + 1]) for s, j in pairs}
    beta_cp = {(s, j): jnp.where(lo, beta_c[s, 2 * j], beta_c[s, 2 * j + 1]) for s, j in pairs}
    gam_rp = {(s, j): jnp.concatenate([gam_t[s][col(s, 2 * j):col(s, 2 * j) + 1, :],
                                       gam_t[s][col(s, 2 * j + 1):col(s, 2 * j + 1) + 1, :]], axis=1)
              for s, j in pairs}
    rel = {p: jnp.where(incl[p[0]], jnp.exp(gam_cp[p] - gam_rp[p]), 0.0) for p in pairs}
    kb = {(s, h): refs[s][1][0, :, sls[h]].astype(BF16) for s, h in heads}
    kd = {(s, j): diag2(kb[s, 2 * j], kb[s, 2 * j + 1]) for s, j in pairs}
    kk = {(s, j): lax.dot_general(jnp.concatenate([kb[s, 2 * j], kb[s, 2 * j + 1]], axis=1), kd[s, j], NT_DIMS,
                                  preferred_element_type=F32) for s, j in pairs}
    qk = {(s, j): lax.dot_general(
        jnp.concatenate([refs[s][0][0, :, sls[2 * j]], refs[s][0][0, :, sls[2 * j + 1]]], axis=1).astype(BF16),
        kd[s, j], NT_DIMS, preferred_element_type=F32) for s, j in pairs}
    qk = {p: jnp.where(incl[p[0]], qk[p] * rel[p], 0.0).astype(BF16) for p in pairs}
    m = {p: jnp.where(strict[p[0]], -(beta_cp[p] * kk[p] * rel[p]), 0.0) for p in pairs}
    tinv = {p: eye + m[p] for p in pairs}
    pw = {p: m[p].astype(BF16) for p in pairs}
    sq = 1
    while 2 * sq < CHUNK:
        pw = {p: jnp.dot(pw[p], diag_packed(pw[p]), preferred_element_type=F32).astype(BF16) for p in pairs}
        tinv = {p: tinv[p] + jnp.dot(tinv[p].astype(BF16), diag_packed(pw[p]), preferred_element_type=F32)
                for p in pairs}
        sq *= 2
    rhs = {}
    for s, h in heads:
        k = refs[s][1][0, :, sls[h]]
        rhs[s, h] = jnp.concatenate([refs[s][2][0, :, sls[h]] * beta_c[s, h],
                                     k * (beta_c[s, h] * jnp.exp(gam_c[s, h]))], axis=1).astype(BF16)
    solp = {(s, j): jnp.dot(tinv[s, j].astype(BF16), diag2(rhs[s, 2 * j], rhs[s, 2 * j + 1]),
                            preferred_element_type=F32) for s, j in pairs}
    sol = {(s, h): solp[s, h // 2][:, (h % 2) * 2 * HEAD_DIM:(h % 2 + 1) * 2 * HEAD_DIM] for s, h in heads}
    stb = {(s, h): st_ref[s, h].astype(BF16) for s, h in heads}
    vnb = {x: (sol[x][:, :HEAD_DIM]
               - lax.dot_general(sol[x][:, HEAD_DIM:].astype(BF16), stb[x], NT_DIMS, preferred_element_type=F32)
               ).astype(BF16) for x in heads}
    intra = {(s, j): jnp.dot(qk[s, j], diag2(vnb[s, 2 * j], vnb[s, 2 * j + 1]), preferred_element_type=F32)
             for s, j in pairs}
    for s, h in heads:
        qdec = (refs[s][0][0, :, sls[h]] * jnp.exp(gam_c[s, h])).astype(BF16)
        refs[s][3][:, sls[h]] = (lax.dot_general(qdec, stb[s, h], NT_DIMS, preferred_element_type=F32)
                                 + intra[s, h // 2][:, (h % 2) * HEAD_DIM:(h % 2 + 1) * HEAD_DIM])
    for s, h in heads:
        last = 0 if s else CHUNK - 1
        gam_last = gam_c[s, h][last:last + 1]
        kdec = (refs[s][1][0, :, sls[h]] * jnp.exp(gam_last - gam_c[s, h])).astype(BF16)
        st_ref[s, h] = st_ref[s, h] * jnp.exp(gam_last) + lax.dot_general(vnb[s, h], kdec, TN_DIMS,
                                                                          preferred_element_type=F32)


def _gdn_scan_bidir(qkv, gab, arow, dtrow, *, nh):
    t = qkv.shape[1]
    w = nh * HEAD_DIM
    nc = t // CHUNK
    fw = lambda i: i
    bw = lambda i: nc - 1 - i
    spec = lambda g, cidx: pl.BlockSpec((1, CHUNK, w), lambda i: (g, cidx(i), 0))
    gspec = lambda cidx: pl.BlockSpec((CHUNK, LANES), lambda i: (cidx(i), 0))
    row = pl.BlockSpec((1, LANES), lambda i: (0, 0))
    out_shape = jax.ShapeDtypeStruct((t, w), F32)
    return pl.pallas_call(
        functools.partial(_gdn_bidir_body, nh=nh),
        grid=(nc,),
        in_specs=[spec(0, fw), spec(1, fw), spec(2, fw), gspec(fw),
                  spec(0, bw), spec(1, bw), spec(2, bw), gspec(bw), row, row],
        out_specs=[pl.BlockSpec((CHUNK, w), lambda i: (fw(i), 0)), pl.BlockSpec((CHUNK, w), lambda i: (bw(i), 0))],
        out_shape=[out_shape, out_shape],
        scratch_shapes=[pltpu.VMEM((2, nh, HEAD_DIM, HEAD_DIM), F32)],
        compiler_params=_params(1),
        name="gdn_bidir",
    )(qkv, qkv, qkv, gab, qkv, qkv, qkv, gab, arow, dtrow)


def _gnorm_body(af_ref, ab_ref, bf_ref, bb_ref, hz_ref, gz_ref, wa_ref, wb_ref, o_ref, *, nh):
    w = nh * HEAD_DIM
    for grp, (f_ref, b_ref, z_ref, w_ref) in enumerate(((af_ref, ab_ref, hz_ref, wa_ref),
                                                        (bf_ref, bb_ref, gz_ref, wb_ref))):
        for h in range(nh):
            sl = slice(h * HEAD_DIM, (h + 1) * HEAD_DIM)
            o = f_ref[:, sl] + b_ref[:, sl]
            y = o * lax.rsqrt(jnp.mean(o * o, axis=1, keepdims=True) + EPS) * w_ref[...]
            z = z_ref[:, sl]
            osl = slice(grp * w + h * HEAD_DIM, grp * w + (h + 1) * HEAD_DIM)
            o_ref[:, osl] = (y * (z * _sigmoid(z))).astype(o_ref.dtype)


def _gated_norm(oa_f, oa_b, ob_f, ob_b, proj, gz, wa, wb, *, nh, ghz):
    t = oa_f.shape[0]
    w = nh * HEAD_DIM
    tm = _pick(t, (256, 128, 64))
    spec = pl.BlockSpec((tm, w), lambda i: (i, 0))
    return pl.pallas_call(
        functools.partial(_gnorm_body, nh=nh),
        grid=(t // tm,),
        in_specs=[spec, spec, spec, spec,
                  pl.BlockSpec((tm, w), lambda i: (i, ghz)),
                  spec,
                  pl.BlockSpec((1, HEAD_DIM), lambda i: (0, 0)),
                  pl.BlockSpec((1, HEAD_DIM), lambda i: (0, 0))],
        out_specs=pl.BlockSpec((tm, 2 * w), lambda i: (i, 0)),
        out_shape=jax.ShapeDtypeStruct((t, 2 * w), BF16),
        compiler_params=_params(1),
        name="gated_head_norm",
    )(oa_f, oa_b, ob_f, ob_b, proj, gz, wa.reshape(1, HEAD_DIM), wb.reshape(1, HEAD_DIM))


def _cand_rows():
    out = []
    for a in range(PEER_TOPK):
        out.append((a, min(PEER_TOPK, PEER_TOPK // (a + 1))))
    return out


N_CAND = sum(nb for _, nb in _cand_rows())
N_CAND_PAD = -(-N_CAND // 8) * 8


def _route_body(qry_ref, sk_ref, c1_ref, e1_ref, r2_ref, e2_ref, top_ref, cand_ref, *, nheads):
    ninf = -jnp.inf
    for h in range(nheads):
        scores, ranks = [], []
        for p in range(2):
            j = 2 * h + p
            qhp = qry_ref[:, j * PEER_NKEYS:(j + 1) * PEER_NKEYS].astype(BF16)
            keys = sk_ref[h, p].astype(BF16)
            s = lax.dot_general(keys, qhp, NT_DIMS, preferred_element_type=F32)
            scores.append(s)
            wv = s
            rk = jnp.full(s.shape, float(PEER_NKEYS), F32)
            for rnk in range(PEER_TOPK):
                mx = jnp.max(wv, axis=0, keepdims=True)
                top_ref[p * PEER_TOPK + rnk:p * PEER_TOPK + rnk + 1, :] = mx
                sel = wv == mx
                if p == 1:
                    rk = jnp.where(sel, float(rnk), rk)
                wv = jnp.where(sel, ninf, wv)
            ranks.append(rk)
        t1 = top_ref[0:PEER_TOPK]
        t2 = top_ref[PEER_TOPK:2 * PEER_TOPK]
        cand_ref[...] = jnp.full(cand_ref.shape, ninf, F32)
        off = 0
        for a, nb in _cand_rows():
            cand_ref[off:off + nb] = t1[a:a + 1] + t2[0:nb]
            off += nb
        cnd = cand_ref[...]
        cmax = t1[0:1] + t2[0:1]
        z = jnp.zeros_like(cmax)
        mx = cmax
        for rnk in range(PEER_TOPK):
            mx = jnp.max(cnd, axis=0, keepdims=True)
            z = z + jnp.exp(mx - cmax)
            cnd = jnp.where(cnd == mx, ninf, cnd)
        s1, s2 = scores
        c1 = jnp.zeros_like(s1)
        for a in range(PEER_TOPK):
            cnt = jnp.sum(jnp.where((t1[a:a + 1] + t2) >= mx, 1.0, 0.0), axis=0, keepdims=True)
            c1 = jnp.where(s1 == t1[a:a + 1], cnt, c1)
        c1_ref[h] = c1
        e1_ref[h] = jnp.exp(s1 - t1[0:1]) / z
        r2_ref[h] = ranks[1].astype(r2_ref.dtype)
        e2_ref[h] = jnp.exp(s2 - t2[0:1]).astype(e2_ref.dtype)


def _peer_route(qry, sub_keys):
    t = qry.shape[0]
    nheads = sub_keys.shape[0]
    tt = _pick(t, (512, 256, 128))
    big = pl.BlockSpec((nheads, PEER_NKEYS, tt), lambda i: (0, 0, i))
    f32_shape = jax.ShapeDtypeStruct((nheads, PEER_NKEYS, t), F32)
    b16_shape = jax.ShapeDtypeStruct((nheads, PEER_NKEYS, t), BF16)
    return pl.pallas_call(
        functools.partial(_route_body, nheads=nheads),
        grid=(t // tt,),
        in_specs=[pl.BlockSpec((tt, qry.shape[1]), lambda i: (i, 0)),
                  pl.BlockSpec(sub_keys.shape, lambda i: (0, 0, 0, 0))],
        out_specs=[big, big, big, big],
        out_shape=[f32_shape, f32_shape, b16_shape, b16_shape],
        scratch_shapes=[pltpu.VMEM((2 * PEER_TOPK, tt), F32), pltpu.VMEM((N_CAND_PAD, tt), F32)],
        compiler_params=_params(1),
        name="peer_route",
    )(qry, sub_keys)


def _peer_body(xn_ref, u_ref, v_ref, c1_ref, e1_ref, r2_ref, e2_ref, o_ref, g_ref, *, nheads, te):
    @pl.when(pl.program_id(1) == 0)
    def _():
        o_ref[...] = jnp.zeros_like(o_ref)

    nrow = te // PEER_NKEYS
    base = (pl.program_id(1) * nrow) % ROW_TILE
    hid = lax.dot_general(u_ref[...], xn_ref[...], NT_DIMS, preferred_element_type=F32)
    zero = jnp.zeros((), g_ref.dtype)
    for ii in range(nrow):
        acc = None
        for h in range(nheads):
            c1row = c1_ref[h, pl.ds(base + ii, 1), :].astype(g_ref.dtype)
            e1row = e1_ref[h, pl.ds(base + ii, 1), :].astype(g_ref.dtype)
            term = jnp.where(r2_ref[h] < c1row, e2_ref[h] * e1row, zero)
            acc = term if acc is None else acc + term
        g_ref[ii * PEER_NKEYS:(ii + 1) * PEER_NKEYS, :] = acc
    act = (0.5 * hid * (1.0 + lax.erf(hid * INV_SQRT2)) * g_ref[...].astype(F32)).astype(BF16)
    o_ref[...] += lax.dot_general(act, v_ref[...], TN_DIMS, preferred_element_type=F32)


def _peer_dense(xn, u, v, c1, e1, r2, e2):
    t, d = xn.shape
    ne = u.shape[0]
    nheads = c1.shape[0]
    tt = _pick(t, (512, 256, 128))
    te = _pick(ne, (512, 256, 128))
    nrow = te // PEER_NKEYS
    big = pl.BlockSpec((nheads, PEER_NKEYS, tt), lambda i, e: (0, 0, i))
    assert ROW_TILE % nrow == 0
    rows = pl.BlockSpec((nheads, ROW_TILE, tt), lambda i, e: (0, (e * nrow) // ROW_TILE, i))
    return pl.pallas_call(
        functools.partial(_peer_body, nheads=nheads, te=te),
        grid=(t // tt, ne // te),
        in_specs=[pl.BlockSpec((tt, d), lambda i, e: (i, 0)),
                  pl.BlockSpec((te, d), lambda i, e: (e, 0)),
                  pl.BlockSpec((te, d), lambda i, e: (e, 0)),
                  rows, rows, big, big],
        out_specs=pl.BlockSpec((tt, d), lambda i, e: (i, 0)),
        out_shape=jax.ShapeDtypeStruct((t, d), F32),
        scratch_shapes=[pltpu.VMEM((te, tt), BF16)],
        compiler_params=_params(2),
        name="peer_dense",
    )(xn, u, v, c1, e1, r2, e2)


def _add_rms_body(a_ref, b_ref, w_ref, s_ref, n_ref):
    x = a_ref[...] + b_ref[...]
    s_ref[...] = x
    ms = jnp.mean(x * x, axis=-1, keepdims=True)
    n_ref[...] = (x * lax.rsqrt(ms + EPS) * w_ref[...]).astype(n_ref.dtype)


def _add_rmsnorm(a, b, w):
    t, d = a.shape
    tm = _pick(t, (256, 128, 64, 8))
    spec = pl.BlockSpec((tm, d), lambda i: (i, 0))
    return pl.pallas_call(
        _add_rms_body,
        grid=(t // tm,),
        in_specs=[spec, spec, pl.BlockSpec((1, d), lambda i: (0, 0))],
        out_specs=[spec, spec],
        out_shape=[jax.ShapeDtypeStruct((t, d), F32), jax.ShapeDtypeStruct((t, d), BF16)],
        compiler_params=_params(1),
        name="add_rmsnorm",
    )(a, b, w.reshape(1, d))


def _ple_body(xn_ref, wg_ref, p_ref, wp_ref, h_ref, pw_ref, fw_ref, o_ref, emb_ref, *, tn):
    j = pl.program_id(1)

    @pl.when(j == 0)
    def _():
        raw = jnp.dot(p_ref[...], wp_ref[...], preferred_element_type=F32)
        ms = jnp.mean(raw * raw, axis=-1, keepdims=True)
        emb_ref[...] = raw * lax.rsqrt(ms + EPS) * pw_ref[...]

    cs = pl.ds(pl.multiple_of(j * tn, tn), tn)
    gate = _sigmoid(jnp.dot(xn_ref[...], wg_ref[...], preferred_element_type=F32))
    o_ref[:, cs] = h_ref[...] + gate * emb_ref[:, cs]

    @pl.when(j == pl.num_programs(1) - 1)
    def _():
        x = o_ref[...]
        ms = jnp.mean(x * x, axis=-1, keepdims=True)
        o_ref[...] = x * lax.rsqrt(ms + EPS) * fw_ref[...]


def _ple_final(xn, w_gate, p, w_proj, h, post_w, final_w):
    t, d = h.shape
    pd = p.shape[1]
    tm = _pick(t, (512, 256, 128, 64, 8))
    tn = _pick(d, (512, 256, 128))
    return pl.pallas_call(
        functools.partial(_ple_body, tn=tn),
        grid=(t // tm, d // tn),
        in_specs=[pl.BlockSpec((tm, d), lambda i, j: (i, 0)),
                  pl.BlockSpec((d, tn), lambda i, j: (0, j)),
                  pl.BlockSpec((tm, pd), lambda i, j: (i, 0)),
                  pl.BlockSpec((pd, d), lambda i, j: (0, 0)),
                  pl.BlockSpec((tm, tn), lambda i, j: (i, j)),
                  pl.BlockSpec((1, d), lambda i, j: (0, 0)),
                  pl.BlockSpec((1, d), lambda i, j: (0, 0))],
        out_specs=pl.BlockSpec((tm, d), lambda i, j: (i, 0)),
        out_shape=jax.ShapeDtypeStruct((t, d), F32),
        scratch_shapes=[pltpu.VMEM((tm, d), F32)],
        compiler_params=_params(2),
        name="ple_final",
    )(xn, w_gate, p, w_proj, h, post_w.reshape(1, d), final_w.reshape(1, d))


def kernel(x, p, attn_norm_w, w_in, hg_lower_bound, gd_conv_w, gd_A_log, gd_dt_bias, hg_out_norm_w, gd_out_norm_w, w_out, ffn_norm_w, peer_w_query, peer_sub_keys, peer_u, peer_v, ple_norm_w, ple_w_gate, ple_w_proj, ple_post_norm_w, final_norm_w):
    bsz, seq, d = x.shape
    depth = w_in.shape[0]
    assert bsz == 1 and depth == 1
    w = d // 2
    nh = w // HEAD_DIM
    assert 4 * nh <= LANES
    t = bsz * seq
    h = x.reshape(t, d)
    i = 0

    n_main = 8 * w
    wi = w_in[i].astype(BF16)
    xn = _rmsnorm(h, attn_norm_w[i])
    proj = _matmul(xn, wi, name="in_proj", ncols=n_main)
    gab = _matmul(xn, wi, name="in_proj_gates", col0=n_main, ncols=LANES)
    gz = _matmul(xn, w_in[i][:, n_main + 4 * nh:].astype(BF16), name="in_proj_gz")

    oa = []
    for dr, rev in ((0, False), (1, True)):
        lbp = hg_lower_bound[:, dr, :].reshape(depth + 1, 1, w)
        oa.append(_hgrn_scan(proj, lbp, nh=nh, gq=0, gf=1 + dr, gi=3, reverse=rev))

    qkv = _conv_qkv(proj, gd_conv_w[i], nh=nh, g0=5)
    pad = jnp.zeros((LANES - 2 * nh,), F32)
    arow = jnp.concatenate([gd_A_log[i].reshape(-1), pad]).reshape(1, LANES)
    dtrow = jnp.concatenate([gd_dt_bias[i].reshape(-1), pad]).reshape(1, LANES)
    ob = _gdn_scan_bidir(qkv, gab, arow, dtrow, nh=nh)

    mix_in = _gated_norm(oa[0], oa[1], ob[0], ob[1], proj, gz, hg_out_norm_w[i], gd_out_norm_w[i],
                         nh=nh, ghz=4)
    h1 = _matmul(mix_in, w_out[i].astype(BF16), residual=h, name="out_proj")

    xn2 = _rmsnorm(h1, ffn_norm_w[i])
    qry = _matmul(xn2, peer_w_query[i].astype(BF16), name="peer_query")
    c1, e1, r2, e2 = _peer_route(qry, peer_sub_keys[i])
    peer_out = _peer_dense(xn2, _to_bf16(peer_u, i), _to_bf16(peer_v, i), c1, e1, r2, e2)
    h2, xn3 = _add_rmsnorm(h1, peer_out, ple_norm_w[i])

    out = _ple_final(xn3, ple_w_gate[i].astype(BF16), p[i].reshape(t, -1).astype(BF16),
                     ple_w_proj[i].astype(BF16), h2, ple_post_norm_w[i], final_norm_w)
    return out.reshape(bsz, seq, d)
```

```python
import functools

import jax
import jax.numpy as jnp
from jax import lax
from jax.experimental import pallas as pl
from jax.experimental.pallas import tpu as pltpu

F32 = jnp.float32
BF16 = jnp.bfloat16
EPS = 1e-6
HEAD_DIM = 128
CHUNK = 64
SUB = 8
HGRN_GROUP = 8
PEER_TOPK = 16
PEER_NKEYS = 128
LANES = 128
ROW_TILE = 8
VMEM_LIMIT_BYTES = 56 * 1024 * 1024
CAST_BLOCK_BYTES = 8 * 1024 * 1024
HIGHEST = lax.Precision.HIGHEST
NT_DIMS = (((1,), (1,)), ((), ()))
TN_DIMS = (((0,), (0,)), ((), ()))
INV_SQRT2 = 0.7071067811865476


def _params(n_axes, **kw):
    return pltpu.CompilerParams(dimension_semantics=("arbitrary",) * n_axes,
                                vmem_limit_bytes=VMEM_LIMIT_BYTES, **kw)


def _sigmoid(x):
    return 1.0 / (1.0 + jnp.exp(-x))


def _pick(n, prefs):
    for t in prefs:
        if n % t == 0:
            return t
    return n


def _rms_body(x_ref, w_ref, o_ref):
    x = x_ref[...]
    ms = jnp.mean(x * x, axis=-1, keepdims=True)
    o_ref[...] = (x * lax.rsqrt(ms + EPS) * w_ref[...]).astype(o_ref.dtype)


def _rmsnorm(x, w):
    t, d = x.shape
    tm = _pick(t, (512, 256, 128, 64, 8))
    return pl.pallas_call(
        _rms_body,
        grid=(t // tm,),
        in_specs=[pl.BlockSpec((tm, d), lambda i: (i, 0)),
                  pl.BlockSpec((1, d), lambda i: (0, 0))],
        out_specs=pl.BlockSpec((tm, d), lambda i: (i, 0)),
        out_shape=jax.ShapeDtypeStruct((t, d), BF16),
        compiler_params=_params(1),
        name="rmsnorm",
    )(x, w.reshape(1, d))


def _cast_body(x_ref, o_ref):
    o_ref[...] = x_ref[0].astype(o_ref.dtype)


def _to_bf16(w3, layer):
    _, m, n = w3.shape
    tm = _pick(m, tuple(c for c in (512, 256, 128, 64, 32, 16, 8) if c * n * 4 <= CAST_BLOCK_BYTES))
    return pl.pallas_call(
        _cast_body,
        grid=(m // tm,),
        in_specs=[pl.BlockSpec((1, tm, n), lambda i: (layer, i, 0))],
        out_specs=pl.BlockSpec((tm, n), lambda i: (i, 0)),
        out_shape=jax.ShapeDtypeStruct((m, n), BF16),
        compiler_params=_params(1),
        name="to_bf16",
    )(w3)


def _mm_body(a_ref, b_ref, o_ref):
    o_ref[...] = jnp.dot(a_ref[...], b_ref[...], preferred_element_type=F32)


def _mm_res_body(a_ref, b_ref, r_ref, o_ref):
    o_ref[...] = r_ref[...] + jnp.dot(a_ref[...], b_ref[...], preferred_element_type=F32)


def _matmul(a, b, residual=None, name="matmul", col0=0, ncols=None):
    m, k = a.shape
    n = b.shape[1] if ncols is None else ncols
    tm = _pick(m, (1024, 512, 256, 128, 64, 8))
    tn = _pick(n, (1024, 512, 256, 128))
    assert col0 % tn == 0
    jb = col0 // tn
    in_specs = [pl.BlockSpec((tm, k), lambda i, j: (i, 0)),
                pl.BlockSpec((k, tn), lambda i, j: (0, jb + j))]
    args = [a, b]
    body = _mm_body
    if residual is not None:
        in_specs.append(pl.BlockSpec((tm, tn), lambda i, j: (i, j)))
        args.append(residual)
        body = _mm_res_body
    return pl.pallas_call(
        body,
        grid=(m // tm, n // tn),
        in_specs=in_specs,
        out_specs=pl.BlockSpec((tm, tn), lambda i, j: (i, j)),
        out_shape=jax.ShapeDtypeStruct((m, n), F32),
        compiler_params=_params(2),
        name=name,
    )(*args)


def _tri(reverse, strict=False):
    r = lax.broadcasted_iota(jnp.int32, (CHUNK, CHUNK), 0)
    c = lax.broadcasted_iota(jnp.int32, (CHUNK, CHUNK), 1)
    if reverse:
        return (c > r) if strict else (c >= r)
    return (c < r) if strict else (c <= r)


def _hgrn_body(hq_ref, hf_ref, hi_ref, lbp_ref, o_ref, st_ref, q_sc, k_sc, b_sc, *, nh, reverse):
    @pl.when(pl.program_id(0) == 0)
    def _():
        st_ref[...] = jnp.zeros_like(st_ref)

    nl = lbp_ref.shape[0]
    mx = lbp_ref[0]
    for l in range(1, nl):
        mx = jnp.maximum(mx, lbp_ref[l])
    den = jnp.exp(lbp_ref[0] - mx)
    e0 = den
    for l in range(1, nl):
        den = den + jnp.exp(lbp_ref[l] - mx)
    lb = e0 / den

    zq = hq_ref[...]
    s = _sigmoid(hf_ref[...])
    q_sc[...] = zq * _sigmoid(zq) * (HEAD_DIM ** -0.5)
    k_sc[...] = (1.0 - lb) * (1.0 - s)
    logf = jnp.log2(lb + (1.0 - lb) * s)
    b_sc[...] = jnp.dot(_tri(reverse).astype(F32), logf, precision=HIGHEST, preferred_element_type=F32)

    row_c = lax.broadcasted_iota(jnp.int32, (CHUNK, CHUNK), 0)
    col_c = lax.broadcasted_iota(jnp.int32, (CHUNK, CHUNK), 1)
    row8 = lax.broadcasted_iota(jnp.int32, (SUB, 1), 0)
    nsub = CHUNK // SUB
    last = 0 if reverse else CHUNK - 1

    for g0 in range(0, nh, HGRN_GROUP):
        heads = list(range(g0, min(g0 + HGRN_GROUP, nh)))
        sls = {h: slice(h * HEAD_DIM, (h + 1) * HEAD_DIM) for h in heads}
        stb = {h: st_ref[h].astype(BF16) for h in heads}
        o = {h: lax.dot_general((q_sc[:, sls[h]] * jnp.exp2(b_sc[:, sls[h]])).astype(BF16), stb[h], NT_DIMS,
                                preferred_element_type=F32) for h in heads}
        cross = {h: None for h in heads}
        bs = CHUNK
        while bs >= 2 * SUB:
            half = bs // 2
            same = (row_c // bs) == (col_c // bs)
            for h in heads:
                qparts, kparts = [], []
                for s0 in range(0, CHUNK, bs):
                    lo_rows, hi_rows = slice(s0, s0 + half), slice(s0 + half, s0 + bs)
                    qs, ks, rr = (lo_rows, hi_rows, s0 + half) if reverse else (hi_rows, lo_rows, s0 + half - 1)
                    ref = b_sc[rr:rr + 1, sls[h]]
                    qpart = q_sc[qs, sls[h]] * jnp.exp2(b_sc[qs, sls[h]] - ref)
                    kpart = k_sc[ks, sls[h]] * jnp.exp2(ref - b_sc[ks, sls[h]])
                    zero = jnp.zeros_like(qpart)
                    qparts += [qpart, zero] if reverse else [zero, qpart]
                    kparts += [zero, kpart] if reverse else [kpart, zero]
                a = lax.dot_general(jnp.concatenate(qparts, axis=0).astype(BF16),
                                    jnp.concatenate(kparts, axis=0).astype(BF16), NT_DIMS,
                                    preferred_element_type=F32)
                if bs < CHUNK:
                    a = jnp.where(same, a, 0.0)
                cross[h] = a if cross[h] is None else cross[h] + a
            bs = half
        for h in heads:
            o[h] = o[h] + jnp.dot(cross[h].astype(BF16), hi_ref[:, sls[h]].astype(BF16),
                                  preferred_element_type=F32)
        for h in heads:
            diag = []
            for j in range(nsub):
                rs = slice(SUB * j, SUB * (j + 1))
                qj = q_sc[rs, sls[h]]
                bj = b_sc[rs, sls[h]]
                oj = jnp.zeros((SUB, HEAD_DIM), F32)
                for s_loc in range(SUB):
                    r = SUB * j + s_loc
                    pr = qj * k_sc[r:r + 1, sls[h]] * jnp.exp2(bj - b_sc[r:r + 1, sls[h]])
                    a = jnp.sum(pr, axis=1, keepdims=True)
                    keep = (row8 <= s_loc) if reverse else (row8 >= s_loc)
                    oj = oj + jnp.where(keep, a, 0.0) * hi_ref[r:r + 1, sls[h]]
                diag.append(oj)
            o_ref[:, sls[h]] = o[h] + jnp.concatenate(diag, axis=0)
        for h in heads:
            b_last = b_sc[last:last + 1, sls[h]]
            kend = (k_sc[:, sls[h]] * jnp.exp2(b_last - b_sc[:, sls[h]])).astype(BF16)
            st_ref[h] = st_ref[h] * jnp.exp2(b_last) + lax.dot_general(
                hi_ref[:, sls[h]].astype(BF16), kend, TN_DIMS, preferred_element_type=F32)


def _hgrn_scan(proj, lbp, *, nh, gq, gf, gi, reverse):
    t = proj.shape[0]
    w = nh * HEAD_DIM
    nc = t // CHUNK
    cidx = (lambda i: nc - 1 - i) if reverse else (lambda i: i)
    spec = lambda g: pl.BlockSpec((CHUNK, w), lambda i: (cidx(i), g))
    return pl.pallas_call(
        functools.partial(_hgrn_body, nh=nh, reverse=reverse),
        grid=(nc,),
        in_specs=[spec(gq), spec(gf), spec(gi),
                  pl.BlockSpec(lbp.shape, lambda i: (0, 0, 0))],
        out_specs=pl.BlockSpec((CHUNK, w), lambda i: (cidx(i), 0)),
        out_shape=jax.ShapeDtypeStruct((t, w), F32),
        scratch_shapes=[pltpu.VMEM((nh, HEAD_DIM, HEAD_DIM), F32)] + [pltpu.VMEM((CHUNK, w), F32)] * 3,
        compiler_params=_params(1),
        name="hgrn2_bw" if reverse else "hgrn2_fw",
    )(proj, proj, proj, lbp)


HALO = 8


def _conv_body(x_ref, xp_ref, xn_ref, w_ref, o_ref, xe_ref, *, nh, tt, kw):
    i = pl.program_id(0)
    g = pl.program_id(1)
    n = pl.num_programs(0)
    xe_ref[0:HALO] = jnp.where(i > 0, xp_ref[...], 0.0)
    xe_ref[HALO:HALO + tt] = x_ref[...]
    xe_ref[HALO + tt:HALO + tt + HALO] = jnp.where(i < n - 1, xn_ref[...], 0.0)
    pad = (kw - 1) // 2
    acc = None
    for j in range(kw):
        term = w_ref[j:j + 1] * xe_ref[HALO - pad + j:HALO - pad + j + tt]
        acc = term if acc is None else acc + term
    y = acc * _sigmoid(acc)
    for h in range(nh):
        sl = slice(h * HEAD_DIM, (h + 1) * HEAD_DIM)
        yh = y[:, sl]
        inv = lax.rsqrt(jnp.sum(yh * yh, axis=1, keepdims=True) + EPS)
        scale = jnp.where(g == 0, inv * (HEAD_DIM ** -0.5), jnp.where(g == 1, inv, 1.0))
        o_ref[0, :, sl] = yh * scale


def _conv_qkv(proj, conv_w, *, nh, g0):
    t = proj.shape[0]
    w = nh * HEAD_DIM
    kw = conv_w.shape[0]
    tt = _pick(t, (256, 128, 64))
    nb = tt // HALO
    last = t // HALO - 1
    return pl.pallas_call(
        functools.partial(_conv_body, nh=nh, tt=tt, kw=kw),
        grid=(t // tt, 3),
        in_specs=[pl.BlockSpec((tt, w), lambda i, g: (i, g0 + g)),
                  pl.BlockSpec((HALO, w), lambda i, g: (jnp.maximum(i * nb - 1, 0), g0 + g)),
                  pl.BlockSpec((HALO, w), lambda i, g: (jnp.minimum((i + 1) * nb, last), g0 + g)),
                  pl.BlockSpec((kw, w), lambda i, g: (0, g))],
        out_specs=pl.BlockSpec((1, tt, w), lambda i, g: (g, i, 0)),
        out_shape=jax.ShapeDtypeStruct((3, t, w), F32),
        scratch_shapes=[pltpu.VMEM((tt + 2 * HALO, w), F32)],
        compiler_params=_params(2),
        name="gdn_conv",
    )(proj, proj, proj, conv_w)


def _gdn_bidir_body(qf_ref, kf_ref, vf_ref, gabf_ref, qb_ref, kb_ref, vb_ref, gabb_ref, arow_ref, dtrow_ref,
                    of_ref, ob_ref, st_ref, *, nh):
    @pl.when(pl.program_id(0) == 0)
    def _():
        st_ref[...] = jnp.zeros_like(st_ref)

    assert nh % 2 == 0 and 2 * CHUNK == LANES
    r = lax.broadcasted_iota(jnp.int32, (CHUNK, LANES), 0)
    lane = lax.broadcasted_iota(jnp.int32, (CHUNK, LANES), 1)
    c = lane & (CHUNK - 1)
    lo = lane < CHUNK
    eye = (r == c).astype(F32)

    def diag2(x, y):
        z = jnp.zeros_like(x)
        return jnp.concatenate([jnp.concatenate([x, z], axis=1), jnp.concatenate([z, y], axis=1)], axis=0)

    def diag_packed(p):
        z = jnp.zeros_like(p)
        return jnp.concatenate([jnp.where(lo, p, z), jnp.where(lo, z, p)], axis=0)

    refs = ((qf_ref, kf_ref, vf_ref, of_ref), (qb_ref, kb_ref, vb_ref, ob_ref))
    gam_all, gam_t, beta_all, incl, strict = [], [], [], [], []
    for s, gab_ref in enumerate((gabf_ref, gabb_ref)):
        gab = gab_ref[...]
        xs = gab + dtrow_ref[...]
        softplus = jnp.maximum(xs, 0.0) + jnp.log(1.0 + jnp.exp(-jnp.abs(xs)))
        dec = -jnp.exp(arow_ref[...]) * softplus
        beta_all.append(_sigmoid(gab))
        ga = jnp.dot(_tri(bool(s)).astype(F32), dec, precision=HIGHEST, preferred_element_type=F32)
        gam_all.append(ga)
        gam_t.append(ga.T)
        incl.append((c >= r) if s else (c <= r))
        strict.append((c > r) if s else (c < r))

    sls = [slice(h * HEAD_DIM, (h + 1) * HEAD_DIM) for h in range(nh)]
    heads = [(s, h) for h in range(nh) for s in (0, 1)]
    pairs = [(s, j) for j in range(nh // 2) for s in (0, 1)]
    col = lambda s, h: s * nh + h
    gam_c = {(s, h): gam_all[s][:, col(s, h):col(s, h) + 1] for s, h in heads}
    beta_c = {(s, h): beta_all[s][:, 2 * nh + col(s, h):2 * nh + col(s, h) + 1] for s, h in heads}
    gam_cp = {(s, j): jnp.where(lo, gam_c[s, 2 * j], gam_c[s, 2 * j + 1]) for s, j in pairs}
    beta_cp = {(s, j): jnp.where(lo, beta_c[s, 2 * j], beta_c[s, 2 * j + 1]) for s, j in pairs}
    gam_rp = {(s, j): jnp.concatenate([gam_t[s][col(s, 2 * j):col(s, 2 * j) + 1, :],
                                       gam_t[s][col(s, 2 * j + 1):col(s, 2 * j + 1) + 1, :]], axis=1)
              for s, j in pairs}
    rel = {p: jnp.where(incl[p[0]], jnp.exp(gam_cp[p] - gam_rp[p]), 0.0) for p in pairs}
    kb = {(s, h): refs[s][1][0, :, sls[h]].astype(BF16) for s, h in heads}
    kd = {(s, j): diag2(kb[s, 2 * j], kb[s, 2 * j + 1]) for s, j in pairs}
    kk = {(s, j): lax.dot_general(jnp.concatenate([kb[s, 2 * j], kb[s, 2 * j + 1]], axis=1), kd[s, j], NT_DIMS,
                                  preferred_element_type=F32) for s, j in pairs}
    qk = {(s, j): lax.dot_general(
        jnp.concatenate([refs[s][0][0, :, sls[2 * j]], refs[s][0][0, :, sls[2 * j + 1]]], axis=1).astype(BF16),
        kd[s, j], NT_DIMS, preferred_element_type=F32) for s, j in pairs}
    qk = {p: jnp.where(incl[p[0]], qk[p] * rel[p], 0.0).astype(BF16) for p in pairs}
    m = {p: jnp.where(strict[p[0]], -(beta_cp[p] * kk[p] * rel[p]), 0.0) for p in pairs}
    tinv = {p: eye + m[p] for p in pairs}
    pw = {p: m[p].astype(BF16) for p in pairs}
    sq = 1
    while 2 * sq < CHUNK:
        pw = {p: jnp.dot(pw[p], diag_packed(pw[p]), preferred_element_type=F32).astype(BF16) for p in pairs}
        tinv = {p: tinv[p] + jnp.dot(tinv[p].astype(BF16), diag_packed(pw[p]), preferred_element_type=F32)
                for p in pairs}
        sq *= 2
    rhs = {}
    for s, h in heads:
        k = refs[s][1][0, :, sls[h]]
        rhs[s, h] = jnp.concatenate([refs[s][2][0, :, sls[h]] * beta_c[s, h],
                                     k * (beta_c[s, h] * jnp.exp(gam_c[s, h]))], axis=1).astype(BF16)
    solp = {(s, j): jnp.dot(tinv[s, j].astype(BF16), diag2(rhs[s, 2 * j], rhs[s, 2 * j + 1]),
                            preferred_element_type=F32) for s, j in pairs}
    sol = {(s, h): solp[s, h // 2][:, (h % 2) * 2 * HEAD_DIM:(h % 2 + 1) * 2 * HEAD_DIM] for s, h in heads}
    stb = {(s, h): st_ref[s, h].astype(BF16) for s, h in heads}
    vnb = {x: (sol[x][:, :HEAD_DIM]
               - lax.dot_general(sol[x][:, HEAD_DIM:].astype(BF16), stb[x], NT_DIMS, preferred_element_type=F32)
               ).astype(BF16) for x in heads}
    intra = {(s, j): jnp.dot(qk[s, j], diag2(vnb[s, 2 * j], vnb[s, 2 * j + 1]), preferred_element_type=F32)
             for s, j in pairs}
    for s, h in heads:
        qdec = (refs[s][0][0, :, sls[h]] * jnp.exp(gam_c[s, h])).astype(BF16)
        refs[s][3][:, sls[h]] = (lax.dot_general(qdec, stb[s, h], NT_DIMS, preferred_element_type=F32)
                                 + intra[s, h // 2][:, (h % 2) * HEAD_DIM:(h % 2 + 1) * HEAD_DIM])
    for s, h in heads:
        last = 0 if s else CHUNK - 1
        gam_last = gam_c[s, h][last:last + 1]
        kdec = (refs[s][1][0, :, sls[h]] * jnp.exp(gam_last - gam_c[s, h])).astype(BF16)
        st_ref[s, h] = st_ref[s, h] * jnp.exp(gam_last) + lax.dot_general(vnb[s, h], kdec, TN_DIMS,
                                                                          preferred_element_type=F32)


def _gdn_scan_bidir(qkv, gab, arow, dtrow, *, nh):
    t = qkv.shape[1]
    w = nh * HEAD_DIM
    nc = t // CHUNK
    fw = lambda i: i
    bw = lambda i: nc - 1 - i
    spec = lambda g, cidx: pl.BlockSpec((1, CHUNK, w), lambda i: (g, cidx(i), 0))
    gspec = lambda cidx: pl.BlockSpec((CHUNK, LANES), lambda i: (cidx(i), 0))
    row = pl.BlockSpec((1, LANES), lambda i: (0, 0))
    out_shape = jax.ShapeDtypeStruct((t, w), F32)
    return pl.pallas_call(
        functools.partial(_gdn_bidir_body, nh=nh),
        grid=(nc,),
        in_specs=[spec(0, fw), spec(1, fw), spec(2, fw), gspec(fw),
                  spec(0, bw), spec(1, bw), spec(2, bw), gspec(bw), row, row],
        out_specs=[pl.BlockSpec((CHUNK, w), lambda i: (fw(i), 0)), pl.BlockSpec((CHUNK, w), lambda i: (bw(i), 0))],
        out_shape=[out_shape, out_shape],
        scratch_shapes=[pltpu.VMEM((2, nh, HEAD_DIM, HEAD_DIM), F32)],
        compiler_params=_params(1),
        name="gdn_bidir",
    )(qkv, qkv, qkv, gab, qkv, qkv, qkv, gab, arow, dtrow)


def _gnorm_body(af_ref, ab_ref, bf_ref, bb_ref, hz_ref, gz_ref, wa_ref, wb_ref, o_ref, *, nh):
    w = nh * HEAD_DIM
    for grp, (f_ref, b_ref, z_ref, w_ref) in enumerate(((af_ref, ab_ref, hz_ref, wa_ref),
                                                        (bf_ref, bb_ref, gz_ref, wb_ref))):
        for h in range(nh):
            sl = slice(h * HEAD_DIM, (h + 1) * HEAD_DIM)
            o = f_ref[:, sl] + b_ref[:, sl]
            y = o * lax.rsqrt(jnp.mean(o * o, axis=1, keepdims=True) + EPS) * w_ref[...]
            z = z_ref[:, sl]
            osl = slice(grp * w + h * HEAD_DIM, grp * w + (h + 1) * HEAD_DIM)
            o_ref[:, osl] = (y * (z * _sigmoid(z))).astype(o_ref.dtype)


def _gated_norm(oa_f, oa_b, ob_f, ob_b, proj, gz, wa, wb, *, nh, ghz):
    t = oa_f.shape[0]
    w = nh * HEAD_DIM
    tm = _pick(t, (256, 128, 64))
    spec = pl.BlockSpec((tm, w), lambda i: (i, 0))
    return pl.pallas_call(
        functools.partial(_gnorm_body, nh=nh),
        grid=(t // tm,),
        in_specs=[spec, spec, spec, spec,
                  pl.BlockSpec((tm, w), lambda i: (i, ghz)),
                  spec,
                  pl.BlockSpec((1, HEAD_DIM), lambda i: (0, 0)),
                  pl.BlockSpec((1, HEAD_DIM), lambda i: (0, 0))],
        out_specs=pl.BlockSpec((tm, 2 * w), lambda i: (i, 0)),
        out_shape=jax.ShapeDtypeStruct((t, 2 * w), BF16),
        compiler_params=_params(1),
        name="gated_head_norm",
    )(oa_f, oa_b, ob_f, ob_b, proj, gz, wa.reshape(1, HEAD_DIM), wb.reshape(1, HEAD_DIM))


def _cand_rows():
    out = []
    for a in range(PEER_TOPK):
        out.append((a, min(PEER_TOPK, PEER_TOPK // (a + 1))))
    return out


N_CAND = sum(nb for _, nb in _cand_rows())
N_CAND_PAD = -(-N_CAND // 8) * 8


def _route_body(qry_ref, sk_ref, c1_ref, e1_ref, r2_ref, e2_ref, top_ref, cand_ref, *, nheads):
    ninf = -jnp.inf
    for h in range(nheads):
        scores, ranks = [], []
        for p in range(2):
            j = 2 * h + p
            qhp = qry_ref[:, j * PEER_NKEYS:(j + 1) * PEER_NKEYS].astype(BF16)
            keys = sk_ref[h, p].astype(BF16)
            s = lax.dot_general(keys, qhp, NT_DIMS, preferred_element_type=F32)
            scores.append(s)
            wv = s
            rk = jnp.full(s.shape, float(PEER_NKEYS), F32)
            for rnk in range(PEER_TOPK):
                mx = jnp.max(wv, axis=0, keepdims=True)
                top_ref[p * PEER_TOPK + rnk:p * PEER_TOPK + rnk + 1, :] = mx
                sel = wv == mx
                if p == 1:
                    rk = jnp.where(sel, float(rnk), rk)
                wv = jnp.where(sel, ninf, wv)
            ranks.append(rk)
        t1 = top_ref[0:PEER_TOPK]
        t2 = top_ref[PEER_TOPK:2 * PEER_TOPK]
        cand_ref[...] = jnp.full(cand_ref.shape, ninf, F32)
        off = 0
        for a, nb in _cand_rows():
            cand_ref[off:off + nb] = t1[a:a + 1] + t2[0:nb]
            off += nb
        cnd = cand_ref[...]
        cmax = t1[0:1] + t2[0:1]
        z = jnp.zeros_like(cmax)
        mx = cmax
        for rnk in range(PEER_TOPK):
            mx = jnp.max(cnd, axis=0, keepdims=True)
            z = z + jnp.exp(mx - cmax)
            cnd = jnp.where(cnd == mx, ninf, cnd)
        s1, s2 = scores
        c1 = jnp.zeros_like(s1)
        for a in range(PEER_TOPK):
            cnt = jnp.sum(jnp.where((t1[a:a + 1] + t2) >= mx, 1.0, 0.0), axis=0, keepdims=True)
            c1 = jnp.where(s1 == t1[a:a + 1], cnt, c1)
        c1_ref[h] = c1
        e1_ref[h] = jnp.exp(s1 - t1[0:1]) / z
        r2_ref[h] = ranks[1].astype(r2_ref.dtype)
        e2_ref[h] = jnp.exp(s2 - t2[0:1]).astype(e2_ref.dtype)


def _peer_route(qry, sub_keys):
    t = qry.shape[0]
    nheads = sub_keys.shape[0]
    tt = _pick(t, (512, 256, 128))
    big = pl.BlockSpec((nheads, PEER_NKEYS, tt), lambda i: (0, 0, i))
    f32_shape = jax.ShapeDtypeStruct((nheads, PEER_NKEYS, t), F32)
    b16_shape = jax.ShapeDtypeStruct((nheads, PEER_NKEYS, t), BF16)
    return pl.pallas_call(
        functools.partial(_route_body, nheads=nheads),
        grid=(t // tt,),
        in_specs=[pl.BlockSpec((tt, qry.shape[1]), lambda i: (i, 0)),
                  pl.BlockSpec(sub_keys.shape, lambda i: (0, 0, 0, 0))],
        out_specs=[big, big, big, big],
        out_shape=[f32_shape, f32_shape, b16_shape, b16_shape],
        scratch_shapes=[pltpu.VMEM((2 * PEER_TOPK, tt), F32), pltpu.VMEM((N_CAND_PAD, tt), F32)],
        compiler_params=_params(1),
        name="peer_route",
    )(qry, sub_keys)


def _peer_body(xn_ref, u_ref, v_ref, c1_ref, e1_ref, r2_ref, e2_ref, o_ref, g_ref, *, nheads, te):
    @pl.when(pl.program_id(1) == 0)
    def _():
        o_ref[...] = jnp.zeros_like(o_ref)

    nrow = te // PEER_NKEYS
    base = (pl.program_id(1) * nrow) % ROW_TILE
    hid = lax.dot_general(u_ref[...], xn_ref[...], NT_DIMS, preferred_element_type=F32)
    zero = jnp.zeros((), g_ref.dtype)
    for ii in range(nrow):
        acc = None
        for h in range(nheads):
            c1row = c1_ref[h, pl.ds(base + ii, 1), :].astype(g_ref.dtype)
            e1row = e1_ref[h, pl.ds(base + ii, 1), :].astype(g_ref.dtype)
            term = jnp.where(r2_ref[h] < c1row, e2_ref[h] * e1row, zero)
            acc = term if acc is None else acc + term
        g_ref[ii * PEER_NKEYS:(ii + 1) * PEER_NKEYS, :] = acc
    act = (0.5 * hid * (1.0 + lax.erf(hid * INV_SQRT2)) * g_ref[...].astype(F32)).astype(BF16)
    o_ref[...] += lax.dot_general(act, v_ref[...], TN_DIMS, preferred_element_type=F32)


def _peer_dense(xn, u, v, c1, e1, r2, e2):
    t, d = xn.shape
    ne = u.shape[0]
    nheads = c1.shape[0]
    tt = _pick(t, (512, 256, 128))
    te = _pick(ne, (512, 256, 128))
    nrow = te // PEER_NKEYS
    big = pl.BlockSpec((nheads, PEER_NKEYS, tt), lambda i, e: (0, 0, i))
    assert ROW_TILE % nrow == 0
    rows = pl.BlockSpec((nheads, ROW_TILE, tt), lambda i, e: (0, (e * nrow) // ROW_TILE, i))
    return pl.pallas_call(
        functools.partial(_peer_body, nheads=nheads, te=te),
        grid=(t // tt, ne // te),
        in_specs=[pl.BlockSpec((tt, d), lambda i, e: (i, 0)),
                  pl.BlockSpec((te, d), lambda i, e: (e, 0)),
                  pl.BlockSpec((te, d), lambda i, e: (e, 0)),
                  rows, rows, big, big],
        out_specs=pl.BlockSpec((tt, d), lambda i, e: (i, 0)),
        out_shape=jax.ShapeDtypeStruct((t, d), F32),
        scratch_shapes=[pltpu.VMEM((te, tt), BF16)],
        compiler_params=_params(2),
        name="peer_dense",
    )(xn, u, v, c1, e1, r2, e2)


def _add_rms_body(a_ref, b_ref, w_ref, s_ref, n_ref):
    x = a_ref[...] + b_ref[...]
    s_ref[...] = x
    ms = jnp.mean(x * x, axis=-1, keepdims=True)
    n_ref[...] = (x * lax.rsqrt(ms + EPS) * w_ref[...]).astype(n_ref.dtype)


def _add_rmsnorm(a, b, w):
    t, d = a.shape
    tm = _pick(t, (256, 128, 64, 8))
    spec = pl.BlockSpec((tm, d), lambda i: (i, 0))
    return pl.pallas_call(
        _add_rms_body,
        grid=(t // tm,),
        in_specs=[spec, spec, pl.BlockSpec((1, d), lambda i: (0, 0))],
        out_specs=[spec, spec],
        out_shape=[jax.ShapeDtypeStruct((t, d), F32), jax.ShapeDtypeStruct((t, d), BF16)],
        compiler_params=_params(1),
        name="add_rmsnorm",
    )(a, b, w.reshape(1, d))


def _ple_body(xn_ref, wg_ref, p_ref, wp_ref, h_ref, pw_ref, fw_ref, o_ref, emb_ref, *, tn):
    j = pl.program_id(1)

    @pl.when(j == 0)
    def _():
        raw = jnp.dot(p_ref[...], wp_ref[...], preferred_element_type=F32)
        ms = jnp.mean(raw * raw, axis=-1, keepdims=True)
        emb_ref[...] = raw * lax.rsqrt(ms + EPS) * pw_ref[...]

    cs = pl.ds(pl.multiple_of(j * tn, tn), tn)
    gate = _sigmoid(jnp.dot(xn_ref[...], wg_ref[...], preferred_element_type=F32))
    o_ref[:, cs] = h_ref[...] + gate * emb_ref[:, cs]

    @pl.when(j == pl.num_programs(1) - 1)
    def _():
        x = o_ref[...]
        ms = jnp.mean(x * x, axis=-1, keepdims=True)
        o_ref[...] = x * lax.rsqrt(ms + EPS) * fw_ref[...]


def _ple_final(xn, w_gate, p, w_proj, h, post_w, final_w):
    t, d = h.shape
    pd = p.shape[1]
    tm = _pick(t, (512, 256, 128, 64, 8))
    tn = _pick(d, (512, 256, 128))
    return pl.pallas_call(
        functools.partial(_ple_body, tn=tn),
        grid=(t // tm, d // tn),
        in_specs=[pl.BlockSpec((tm, d), lambda i, j: (i, 0)),
                  pl.BlockSpec((d, tn), lambda i, j: (0, j)),
                  pl.BlockSpec((tm, pd), lambda i, j: (i, 0)),
                  pl.BlockSpec((pd, d), lambda i, j: (0, 0)),
                  pl.BlockSpec((tm, tn), lambda i, j: (i, j)),
                  pl.BlockSpec((1, d), lambda i, j: (0, 0)),
                  pl.BlockSpec((1, d), lambda i, j: (0, 0))],
        out_specs=pl.BlockSpec((tm, d), lambda i, j: (i, 0)),
        out_shape=jax.ShapeDtypeStruct((t, d), F32),
        scratch_shapes=[pltpu.VMEM((tm, d), F32)],
        compiler_params=_params(2),
        name="ple_final",
    )(xn, w_gate, p, w_proj, h, post_w.reshape(1, d), final_w.reshape(1, d))


def kernel(x, p, attn_norm_w, w_in, hg_lower_bound, gd_conv_w, gd_A_log, gd_dt_bias, hg_out_norm_w, gd_out_norm_w, w_out, ffn_norm_w, peer_w_query, peer_sub_keys, peer_u, peer_v, ple_norm_w, ple_w_gate, ple_w_proj, ple_post_norm_w, final_norm_w):
    bsz, seq, d = x.shape
    depth = w_in.shape[0]
    assert bsz == 1 and depth == 1
    w = d // 2
    nh = w // HEAD_DIM
    assert 4 * nh <= LANES
    t = bsz * seq
    h = x.reshape(t, d)
    i = 0

    n_main = 8 * w
    wi = w_in[i].astype(BF16)
    xn = _rmsnorm(h, attn_norm_w[i])
    proj = _matmul(xn, wi, name="in_proj", ncols=n_main)
    gab = _matmul(xn, wi, name="in_proj_gates", col0=n_main, ncols=LANES)
    gz = _matmul(xn, w_in[i][:, n_main + 4 * nh:].astype(BF16), name="in_proj_gz")

    oa = []
    for dr, rev in ((0, False), (1, True)):
        lbp = hg_lower_bound[:, dr, :].reshape(depth + 1, 1, w)
        oa.append(_hgrn_scan(proj, lbp, nh=nh, gq=0, gf=1 + dr, gi=3, reverse=rev))

    qkv = _conv_qkv(proj, gd_conv_w[i], nh=nh, g0=5)
    pad = jnp.zeros((LANES - 2 * nh,), F32)
    arow = jnp.concatenate([gd_A_log[i].reshape(-1), pad]).reshape(1, LANES)
    dtrow = jnp.concatenate([gd_dt_bias[i].reshape(-1), pad]).reshape(1, LANES)
    ob = _gdn_scan_bidir(qkv, gab, arow, dtrow, nh=nh)

    mix_in = _gated_norm(oa[0], oa[1], ob[0], ob[1], proj, gz, hg_out_norm_w[i], gd_out_norm_w[i],
                         nh=nh, ghz=4)
    h1 = _matmul(mix_in, w_out[i].astype(BF16), residual=h, name="out_proj")

    xn2 = _rmsnorm(h1, ffn_norm_w[i])
    qry = _matmul(xn2, peer_w_query[i].astype(BF16), name="peer_query")
    c1, e1, r2, e2 = _peer_route(qry, peer_sub_keys[i])
    peer_out = _peer_dense(xn2, _to_bf16(peer_u, i), _to_bf16(peer_v, i), c1, e1, r2, e2)
    h2, xn3 = _add_rmsnorm(h1, peer_out, ple_norm_w[i])

    out = _ple_final(xn3, ple_w_gate[i].astype(BF16), p[i].reshape(t, -1).astype(BF16),
                     ple_w_proj[i].astype(BF16), h2, ple_post_norm_w[i], final_norm_w)
    return out.reshape(bsz, seq, d)
```
